```python
import math
import jax
import jax.numpy as jnp
from jax import lax
import numpy as np

D_MODEL = 2048
BATCH = 16
SEQ = 256
DEPTH = 4
DEC_BATCH = 8
DEC_SEQ = 4096
PAST_LEN = 512

GRID_W = 64
HEAD_DIM = 128
DIFF_HALF = HEAD_DIM // 2
H_A = 8
HKV_A = 2
H_B = 8
H_C = 8
H_D = 8
HKV_D = 2
KV_HEADS = HKV_A + H_B
D_FF = 5632
QBLOCK = 128
NA_ROWS = 8
NA_COLS = 16
WINDOW = 128
ROPE_THETA = 10000.0
LN_EPS = 1e-6
N_MOD = 9
DN_ALPHA = (2 * DEPTH) ** 0.25
DN_BETA = (8 * DEPTH) ** -0.25
N_EVEN = (DEPTH + 1) // 2
N_ODD = DEPTH // 2
EVEN_WIDTHS = (H_A * HEAD_DIM, HKV_A * HEAD_DIM, HKV_A * HEAD_DIM, H_B * HEAD_DIM, H_B * HEAD_DIM, H_B * HEAD_DIM)
ODD_WIDTHS = (H_C * HEAD_DIM, H_C * HEAD_DIM, H_C * HEAD_DIM, H_D * HEAD_DIM, HKV_D * HEAD_DIM, HKV_D * HEAD_DIM)
PROJ_W = H_A * HEAD_DIM + 2 * HKV_A * HEAD_DIM + 3 * H_B * HEAD_DIM
MIX_W = (H_A + H_B) * HEAD_DIM

kernel_name = "hybrid_diffusion_prefix_trunk_step"


def _layernorm(x, g, b):
    xf = x.astype(jnp.float32)
    mu = jnp.mean(xf, -1, keepdims=True)
    var = jnp.mean(jnp.square(xf - mu), -1, keepdims=True)
    return ((xf - mu) * lax.rsqrt(var + LN_EPS) * g + b).astype(x.dtype)


def _rmsnorm(x, g):
    xf = x.astype(jnp.float32)
    return (xf * lax.rsqrt(jnp.mean(jnp.square(xf), -1, keepdims=True) + LN_EPS) * g).astype(x.dtype)


def _rope_1d(x, pos):
    half = x.shape[-1] // 2
    freqs = ROPE_THETA ** (-jnp.arange(half, dtype=jnp.float32) / half)
    ang = pos.astype(jnp.float32)[:, None] * freqs[None, :]
    cos, sin = jnp.cos(ang), jnp.sin(ang)
    xf = x.astype(jnp.float32)
    x1, x2 = xf[..., :half], xf[..., half:]
    return jnp.concatenate([x1 * cos - x2 * sin, x2 * cos + x1 * sin], -1).astype(x.dtype)


def _rope_axial(x, row, col):
    h = x.shape[-1] // 2
    return jnp.concatenate([_rope_1d(x[..., :h], row), _rope_1d(x[..., h:], col)], -1)


def _rope_diff(x, row, col):
    return jnp.concatenate([_rope_axial(x[..., :DIFF_HALF], row, col),
                            _rope_axial(x[..., DIFF_HALF:], row, col)], -1)


def _split_cols(proj, widths):
    offs = np.cumsum([0] + list(widths))
    return [proj[..., int(offs[i]):int(offs[i + 1])] for i in range(len(widths))]


def _heads(t, h):
    bsz, n, _ = t.shape
    return t.reshape(bsz, n, h, HEAD_DIM).transpose(0, 2, 1, 3)


def _merge(o):
    bsz, h, n, dh = o.shape
    return o.transpose(0, 2, 1, 3).reshape(bsz, n, h * dh)


def _swiglu(h, w1, w3, w2):
    return (jax.nn.silu(h @ w1) * (h @ w3)) @ w2


def _blocked(fn, q):
    *lead, n, dh = q.shape
    nb = n // QBLOCK
    qb = jnp.moveaxis(q.reshape(*lead, nb, QBLOCK, dh), -3, 0)
    out = lax.map(fn, qb)
    out = jnp.moveaxis(out, 0, -3)
    return out.reshape(*lead, n, out.shape[-1])


def _gqa_attn(q, k, v, sink=None):
    scale = q.shape[-1] ** -0.5

    def blk(qb):
        s = jnp.einsum('bhgqd,bhkd->bhgqk', qb, k, preferred_element_type=jnp.float32) * scale
        if sink is not None:
            sk = jnp.broadcast_to(sink.astype(jnp.float32)[None, :, :, None, None], s.shape[:-1] + (1,))
            p = jax.nn.softmax(jnp.concatenate([s, sk], -1), -1)[..., :-1]
        else:
            p = jax.nn.softmax(s, -1)
        return jnp.einsum('bhgqk,bhkd->bhgqd', p.astype(v.dtype), v)

    return _blocked(blk, q)


def _diff_attn(q, k, v, lam):
    scale = DIFF_HALF ** -0.5
    k1, k2 = k[..., :DIFF_HALF], k[..., DIFF_HALF:]

    def blk(qb):
        s1 = jnp.einsum('bhqd,bhkd->bhqk', qb[..., :DIFF_HALF], k1, preferred_element_type=jnp.float32) * scale
        s2 = jnp.einsum('bhqd,bhkd->bhqk', qb[..., DIFF_HALF:], k2, preferred_element_type=jnp.float32) * scale
        wts = jax.nn.softmax(s1, -1) - lam * jax.nn.softmax(s2, -1)
        return jnp.einsum('bhqk,bhkd->bhqd', wts.astype(v.dtype), v)

    return _blocked(blk, q)


def _banded_sink_attn(q, k, v, kc, vc, sink):
    bsz, hkv, g, n, dh = q.shape
    nb = n // WINDOW
    scale = dh ** -0.5

    def band(t):
        tb = t.reshape(bsz, hkv, nb, WINDOW, dh)
        tp = jnp.pad(tb, ((0, 0), (0, 0), (1, 1), (0, 0), (0, 0)))
        return jnp.concatenate([tp[:, :, :-2], tp[:, :, 1:-1], tp[:, :, 2:]], axis=3)

    kb, vb = band(k), band(v)
    qb = q.reshape(bsz, hkv, g, nb, WINDOW, dh)
    s_loc = jnp.einsum('bhgnqd,bhnkd->bhgnqk', qb, kb, preferred_element_type=jnp.float32) * scale
    blk = jnp.arange(nb)[:, None, None] * WINDOW
    qpos = blk + jnp.arange(WINDOW)[None, :, None]
    kpos = blk - WINDOW + jnp.arange(3 * WINDOW)[None, None, :]
    valid = (kpos >= 0) & (kpos < n) & (jnp.abs(qpos - kpos) <= WINDOW)
    s_loc = jnp.where(valid, s_loc, -jnp.inf)
    s_ctx = jnp.einsum('bhgnqd,bhmd->bhgnqm', qb, kc, preferred_element_type=jnp.float32) * scale
    sk = jnp.broadcast_to(sink.astype(jnp.float32)[None, :, :, None, None, None], s_ctx.shape[:-1] + (1,))
    p = jax.nn.softmax(jnp.concatenate([s_loc, s_ctx, sk], -1), -1)
    p_loc = p[..., :3 * WINDOW]
    p_ctx = p[..., 3 * WINDOW:-1]
    o = (jnp.einsum('bhgnqk,bhnkd->bhgnqd', p_loc.astype(v.dtype), vb)
         + jnp.einsum('bhgnqm,bhmd->bhgnqd', p_ctx.astype(vc.dtype), vc))
    return o.reshape(bsz, hkv, g, n, dh)


def _na2d_attn(q, k, v, kc, vc, rpb):
    bsz, h, n, dh = q.shape
    rows = n // GRID_W
    kr_n = min(NA_ROWS, rows)
    scale = dh ** -0.5
    col = jnp.arange(GRID_W)
    col_start = jnp.clip(col - NA_COLS // 2, 0, GRID_W - NA_COLS)
    col_idx = col_start[:, None] + jnp.arange(NA_COLS)[None, :]
    col_off = col_idx - col[:, None]
    qg = q.reshape(bsz, h, rows, GRID_W, dh)
    kg = k.reshape(bsz, h, rows, GRID_W, dh)
    vg = v.reshape(bsz, h, rows, GRID_W, dh)
    rpb_f = rpb.astype(jnp.float32)

    def one_row(args):
        q_r, r = args
        rs = jnp.clip(r - kr_n // 2, 0, rows - kr_n)
        k_r = lax.dynamic_slice_in_dim(kg, rs, kr_n, axis=2)[:, :, :, col_idx]
        v_r = lax.dynamic_slice_in_dim(vg, rs, kr_n, axis=2)[:, :, :, col_idx]
        row_off = rs + jnp.arange(kr_n) - r
        bias = rpb_f[:, row_off[:, None, None] + NA_ROWS - 1, col_off[None] + NA_COLS - 1]
        s_loc = (jnp.einsum('bhwd,bhrwcd->bhwrc', q_r, k_r, preferred_element_type=jnp.float32) * scale
                 + jnp.transpose(bias, (0, 2, 1, 3))[None])
        s_loc = s_loc.reshape(bsz, h, GRID_W, kr_n * NA_COLS)
        s_ctx = jnp.einsum('bhwd,bhmd->bhwm', q_r, kc, preferred_element_type=jnp.float32) * scale
        p = jax.nn.softmax(jnp.concatenate([s_loc, s_ctx], -1), -1)
        p_loc = p[..., :kr_n * NA_COLS].reshape(bsz, h, GRID_W, kr_n, NA_COLS)
        p_ctx = p[..., kr_n * NA_COLS:]
        return (jnp.einsum('bhwrc,bhrwcd->bhwd', p_loc.astype(v.dtype), v_r)
                + jnp.einsum('bhwm,bhmd->bhwd', p_ctx.astype(vc.dtype), vc))

    out = lax.map(one_row, (jnp.moveaxis(qg, 2, 0), jnp.arange(rows)))
    return jnp.moveaxis(out, 0, 2).reshape(bsz, h, n, dh)


def _even_mixer(proj, li, w, pos, ctx_k, ctx_v):
    e = li // 2
    lam_init = 0.8 - 0.6 * math.exp(-0.3 * li)
    q_a, k_a, v_a, q_b, k_b, v_b = _split_cols(proj, EVEN_WIDTHS)
    q_a = _rmsnorm(_heads(q_a, H_A), w["qk_gain_a"][e, 0])
    k_a = _rmsnorm(_heads(k_a, HKV_A), w["qk_gain_a"][e, 1])
    v_a = _heads(v_a, HKV_A)
    q_b, k_b, v_b = _heads(q_b, H_B), _heads(k_b, H_B), _heads(v_b, H_B)
    lb = w["lam_b"][e].astype(jnp.float32)
    lam = jnp.exp(jnp.sum(lb[0] * lb[1])) - jnp.exp(jnp.sum(lb[2] * lb[3])) + lam_init
    if pos is None:
        new_k = jnp.concatenate([k_a, k_b], 1)
        new_v = jnp.concatenate([v_a, v_b], 1)
        ka_all, va_all, kb_all, vb_all = k_a, v_a, k_b, v_b
    else:
        row, col = pos
        q_a, k_a = _rope_axial(q_a, row, col), _rope_axial(k_a, row, col)
        q_b, k_b = _rope_diff(q_b, row, col), _rope_diff(k_b, row, col)
        new_k, new_v = None, None
        ka_all = jnp.concatenate([k_a, ctx_k[:, :HKV_A]], 2)
        va_all = jnp.concatenate([v_a, ctx_v[:, :HKV_A]], 2)
        kb_all = jnp.concatenate([k_b, ctx_k[:, HKV_A:]], 2)
        vb_all = jnp.concatenate([v_b, ctx_v[:, HKV_A:]], 2)
    bsz, _, n, _ = q_a.shape
    o_a = _gqa_attn(q_a.reshape(bsz, HKV_A, H_A // HKV_A, n, HEAD_DIM), ka_all, va_all)
    o_a = o_a.reshape(bsz, H_A, n, HEAD_DIM)
    o_b = _diff_attn(q_b, kb_all, vb_all, lam)
    o_b = _rmsnorm(o_b, w["subln_b"][e]) * (1.0 - lam_init)
    return jnp.concatenate([_merge(o_a), _merge(o_b)], -1), new_k, new_v


def _odd_mixer(proj, li, w, pos, ctx_k, ctx_v):
    o_i = li // 2
    q_c, k_c, v_c, q_d, k_d, v_d = _split_cols(proj, ODD_WIDTHS)
    q_c, k_c, v_c = _heads(q_c, H_C), _heads(k_c, H_C), _heads(v_c, H_C)
    q_d, k_d, v_d = _heads(q_d, H_D), _heads(k_d, HKV_D), _heads(v_d, HKV_D)
    sink = w["sink_d"][o_i].reshape(HKV_D, H_D // HKV_D)
    bsz, _, n, _ = q_c.shape
    if pos is None:
        new_k = jnp.concatenate([k_c, k_d], 1)
        new_v = jnp.concatenate([v_c, v_d], 1)
        o_c = _gqa_attn(q_c[:, :, None], k_c, v_c)[:, :, 0]
        o_d = _gqa_attn(q_d.reshape(bsz, HKV_D, H_D // HKV_D, n, HEAD_DIM), k_d, v_d, sink)
    else:
        row, col = pos
        q_d, k_d = _rope_axial(q_d, row, col), _rope_axial(k_d, row, col)
        new_k, new_v = None, None
        o_c = _na2d_attn(q_c, k_c, v_c, ctx_k[:, :H_C], ctx_v[:, :H_C], w["rpb_c"][o_i])
        o_d = _banded_sink_attn(q_d.reshape(bsz, HKV_D, H_D // HKV_D, n, HEAD_DIM), k_d, v_d,
                                ctx_k[:, H_C:], ctx_v[:, H_C:], sink)
    o_d = o_d.reshape(bsz, H_D, n, HEAD_DIM)
    return jnp.concatenate([_merge(o_c), _merge(o_d)], -1), new_k, new_v


def _layer(x, cond, li, w, pos, ctx_k, ctx_v):
    mods = jax.nn.silu(cond) @ w["w_ada"][li] + w["b_ada"][li]
    sh1, sc1, g1, sh2, sc2, g2, sh3, sc3, g3 = [m[:, None, :] for m in jnp.split(mods, N_MOD, axis=-1)]
    h = x * (1 + sc1) + sh1
    f1 = _swiglu(h, w["ffn_w1"][li, 0], w["ffn_w3"][li, 0], w["ffn_w2"][li, 0])
    x = _layernorm(DN_ALPHA * x + 0.5 * g1 * f1, w["ln_g"][li, 0], w["ln_b"][li, 0])
    h = x * (1 + sc2) + sh2
    proj = h @ w["w_in"][li]
    if li % 2 == 0:
        mix, new_k, new_v = _even_mixer(proj, li, w, pos, ctx_k, ctx_v)
    else:
        mix, new_k, new_v = _odd_mixer(proj, li, w, pos, ctx_k, ctx_v)
    x = _layernorm(DN_ALPHA * x + g2 * (mix @ w["w_out"][li]), w["ln_g"][li, 1], w["ln_b"][li, 1])
    h = x * (1 + sc3) + sh3
    f2 = _swiglu(h, w["ffn_w1"][li, 1], w["ffn_w3"][li, 1], w["ffn_w2"][li, 1])
    x = _layernorm(DN_ALPHA * x + 0.5 * g3 * f2, w["ln_g"][li, 2], w["ln_b"][li, 2])
    return x, new_k, new_v


def setup_inputs(seed: int = 0) -> dict:
    key = jax.random.key(seed)
    ks = jax.random.split(key, 20)

    def nrm(k, shape, s):
        return jax.random.normal(k, shape, jnp.float32) * s

    d_in = D_MODEL ** -0.5
    return {
        "x_prompt": nrm(ks[0], (BATCH, SEQ, D_MODEL), 1.0),
        "x_sample": nrm(ks[1], (DEC_BATCH, DEC_SEQ, D_MODEL), 1.0),
        "cache_k": nrm(ks[2], (DEC_BATCH, DEPTH, KV_HEADS, PAST_LEN, HEAD_DIM), 1.0),
        "cache_v": nrm(ks[3], (DEC_BATCH, DEPTH, KV_HEADS, PAST_LEN, HEAD_DIM), 1.0),
        "c": nrm(ks[4], (DEC_BATCH, D_MODEL), 1.0),
        "c_ctx": nrm(ks[5], (D_MODEL,), 1.0),
        "w_ada": nrm(ks[6], (DEPTH, D_MODEL, N_MOD * D_MODEL), 0.5 * d_in),
        "b_ada": nrm(ks[7], (DEPTH, N_MOD * D_MODEL), 0.01),
        "ln_g": 1.0 + nrm(ks[8], (DEPTH, 3, D_MODEL), 0.02),
        "ln_b": nrm(ks[9], (DEPTH, 3, D_MODEL), 0.02),
        "ffn_w1": nrm(ks[10], (DEPTH, 2, D_MODEL, D_FF), d_in),
        "ffn_w3": nrm(ks[11], (DEPTH, 2, D_MODEL, D_FF), d_in),
        "ffn_w2": nrm(ks[12], (DEPTH, 2, D_FF, D_MODEL), DN_BETA * D_FF ** -0.5),
        "w_in": nrm(ks[13], (DEPTH, D_MODEL, PROJ_W), d_in),
        "w_out": nrm(ks[14], (DEPTH, MIX_W, D_MODEL), DN_BETA * MIX_W ** -0.5),
        "qk_gain_a": 1.0 + nrm(ks[15], (N_EVEN, 2, HEAD_DIM), 0.02),
        "lam_b": nrm(ks[16], (N_EVEN, 4, DIFF_HALF), 0.1),
        "subln_b": 1.0 + nrm(ks[17], (N_EVEN, HEAD_DIM), 0.02),
        "rpb_c": nrm(ks[18], (N_ODD, H_C, 2 * NA_ROWS - 1, 2 * NA_COLS - 1), 0.1),
        "sink_d": nrm(ks[19], (N_ODD, H_D), 0.5),
    }


def reference(x_prompt, x_sample, cache_k, cache_v, c, c_ctx, w_ada, b_ada, ln_g, ln_b,
              ffn_w1, ffn_w3, ffn_w2, w_in, w_out, qk_gain_a, lam_b, subln_b, rpb_c, sink_d):
    w = {"w_ada": w_ada, "b_ada": b_ada, "ln_g": ln_g, "ln_b": ln_b,
         "ffn_w1": ffn_w1, "ffn_w3": ffn_w3, "ffn_w2": ffn_w2, "w_in": w_in, "w_out": w_out,
         "qk_gain_a": qk_gain_a, "lam_b": lam_b, "subln_b": subln_b, "rpb_c": rpb_c, "sink_d": sink_d}
    xp = x_prompt
    ks_list, vs_list = [], []
    for li in range(DEPTH):
        xp, kc, vc = _layer(xp, c_ctx[None, :], li, w, None, None, None)
        ks_list.append(kc)
        vs_list.append(vc)
    new_cache_k = jnp.stack(ks_list, axis=1)
    new_cache_v = jnp.stack(vs_list, axis=1)
    n = x_sample.shape[1]
    t = jnp.arange(n, dtype=jnp.int32)
    pos = (t // GRID_W, t % GRID_W)
    xs = x_sample
    for li in range(DEPTH):
        xs, _, _ = _layer(xs, c, li, w, pos, cache_k[:, li], cache_v[:, li])
    return (xp, xs, new_cache_k, new_cache_v)
```

```python
import functools
import math

import jax
import jax.numpy as jnp
import numpy as np
from jax import lax
from jax.experimental import pallas as pl
from jax.experimental.pallas import tpu as pltpu

D_MODEL = 2048
DEPTH = 4
GRID_W = 64
HEAD_DIM = 128
DIFF_HALF = HEAD_DIM // 2
H_A = 8
HKV_A = 2
H_B = 8
H_C = 8
H_D = 8
HKV_D = 2
KV_HEADS = HKV_A + H_B
D_FF = 5632
NA_ROWS = 8
NA_COLS = 16
WINDOW = 128
ROPE_THETA = 10000.0
LN_EPS = 1e-6
N_MOD = 9
DN_ALPHA = (2 * DEPTH) ** 0.25
PROJ_W = H_A * HEAD_DIM + 2 * HKV_A * HEAD_DIM + 3 * H_B * HEAD_DIM
N_PROJ_HEADS = PROJ_W // HEAD_DIM
MIX_HALF = H_A * HEAD_DIM

V7X_VMEM_BYTES = 64 * 1024 * 1024
VMEM_LIMIT_BYTES = V7X_VMEM_BYTES - 8 * 1024 * 1024
LANES = 128
MASK_VALUE = -1e30

BF16 = jnp.bfloat16
F32 = jnp.float32


def _params(*semantics):
    return pltpu.CompilerParams(dimension_semantics=semantics, vmem_limit_bytes=VMEM_LIMIT_BYTES)


def _dot(a, b):
    return jnp.dot(a, b, preferred_element_type=F32)


def _dot_nt(a, b):
    return lax.dot_general(a, b, (((1,), (1,)), ((), ())), preferred_element_type=F32)


def _layernorm(y, g, b):
    mu = jnp.mean(y, axis=-1, keepdims=True)
    yc = y - mu
    var = jnp.mean(yc * yc, axis=-1, keepdims=True)
    return yc * lax.rsqrt(var + LN_EPS) * g + b


def _rmsnorm(t, g):
    return t * lax.rsqrt(jnp.mean(t * t, axis=-1, keepdims=True) + LN_EPS) * g


def _mods_kernel(c_ref, w_ref, b_ref, o_ref):
    c = c_ref[...]
    s = (c / (1.0 + jnp.exp(-c))).astype(BF16)
    o_ref[...] = _dot(s, w_ref[...].astype(BF16)) + b_ref[...]


def _mods_call(cond, w_ada, b_ada, tn=1024):
    rows = cond.shape[0]
    width = N_MOD * D_MODEL
    return pl.pallas_call(
        _mods_kernel,
        out_shape=jax.ShapeDtypeStruct((DEPTH, rows, width), F32),
        grid=(DEPTH, width // tn),
        in_specs=[
            pl.BlockSpec((rows, D_MODEL), lambda l, j: (0, 0)),
            pl.BlockSpec((None, D_MODEL, tn), lambda l, j: (l, 0, j)),
            pl.BlockSpec((None, 1, tn), lambda l, j: (l, 0, j)),
        ],
        out_specs=pl.BlockSpec((None, rows, tn), lambda l, j: (l, 0, j)),
        compiler_params=_params("parallel", "parallel"),
        name="adaln_mods",
    )(cond, w_ada, b_ada.reshape(DEPTH, 1, width))


def _ffn_kernel(x_ref, mod_ref, w1_ref, w3_ref, w2_ref, g_ref, b_ref, o_ref, h_ref, *, mod_base):
    j = pl.program_id(2)

    @pl.when(j == 0)
    def _():
        shift = mod_ref[0, mod_base:mod_base + 1, :]
        scale = mod_ref[0, mod_base + 1:mod_base + 2, :]
        h_ref[...] = (x_ref[0] * (1.0 + scale) + shift).astype(BF16)

    h = h_ref[...]
    a = _dot(h, w1_ref[...])
    b = _dot(h, w3_ref[...])
    u = (a / (1.0 + jnp.exp(-a)) * b).astype(BF16)
    part = _dot(u, w2_ref[...])

    @pl.when(j == 0)
    def _():
        o_ref[0] = part

    @pl.when(j > 0)
    def _():
        o_ref[0] += part

    @pl.when(j == pl.num_programs(2) - 1)
    def _():
        gate = mod_ref[0, mod_base + 2:mod_base + 3, :]
        y = DN_ALPHA * x_ref[0] + (0.5 * gate) * o_ref[0]
        o_ref[0] = _layernorm(y, g_ref[...], b_ref[...])


def _ffn_call(x, mods, w1, w3, w2, ln_g, ln_b, li, slot, tm=512, tf=512):
    nb, n, d = x.shape
    mod_base = 0 if slot == 0 else 6
    ln_idx = 0 if slot == 0 else 2
    g = ln_g[li, ln_idx][None, :]
    b = ln_b[li, ln_idx][None, :]
    return pl.pallas_call(
        functools.partial(_ffn_kernel, mod_base=mod_base),
        out_shape=jax.ShapeDtypeStruct(x.shape, F32),
        grid=(nb, n // tm, w1.shape[-1] // tf),
        in_specs=[
            pl.BlockSpec((1, tm, d), lambda bb, i, j: (bb, i, 0)),
            pl.BlockSpec((1, N_MOD, d), lambda bb, i, j: (bb, 0, 0)),
            pl.BlockSpec((None, None, d, tf), lambda bb, i, j: (li, slot, 0, j)),
            pl.BlockSpec((None, None, d, tf), lambda bb, i, j: (li, slot, 0, j)),
            pl.BlockSpec((None, None, tf, d), lambda bb, i, j: (li, slot, j, 0)),
            pl.BlockSpec((1, d), lambda bb, i, j: (0, 0)),
            pl.BlockSpec((1, d), lambda bb, i, j: (0, 0)),
        ],
        out_specs=pl.BlockSpec((1, tm, d), lambda bb, i, j: (bb, i, 0)),
        scratch_shapes=[pltpu.VMEM((tm, d), BF16)],
        compiler_params=_params("parallel", "parallel", "arbitrary"),
        name="ffn",
    )(x, mods, w1, w3, w2, g, b)


_EVEN_GROUPS = ((H_A, 0, "axial"), (HKV_A, 1, "axial"), (HKV_A, None, None),
                (H_B, None, "diff"), (H_B, None, "diff"), (H_B, None, None))
_ODD_GROUPS = ((H_C, None, None), (H_C, None, None), (H_C, None, None),
               (H_D, None, "axial"), (HKV_D, None, "axial"), (HKV_D, None, None))
_EVEN_CACHE = (tuple(range(8, 10)) + tuple(range(20, 28)), tuple(range(10, 12)) + tuple(range(28, 36)))
_ODD_CACHE = (tuple(range(8, 16)) + tuple(range(32, 34)), tuple(range(16, 24)) + tuple(range(34, 36)))


def _rotate(t, rope_ref, kind):
    base, half = (0, 32) if kind == "axial" else (3, 16)
    up = pltpu.roll(t, LANES - half, 1)
    down = pltpu.roll(t, half, 1)
    return t * rope_ref[base] + up * rope_ref[base + 1] + down * rope_ref[base + 2]


def _proj_kernel(*refs, groups, use_rope, cache_heads, seq):
    x_ref, mod_ref, w_ref, gain_ref = refs[:4]
    pos = 4
    rope_ref = None
    if use_rope:
        rope_ref = refs[pos]
        pos += 1
    qkv_ref = refs[pos]
    pos += 1
    ck_ref = cv_ref = None
    if cache_heads is not None:
        ck_ref, cv_ref = refs[pos], refs[pos + 1]

    shift = mod_ref[0, 3:4, :]
    scale = mod_ref[0, 4:5, :]
    h = (x_ref[0] * (1.0 + scale) + shift).astype(BF16)
    tm = h.shape[0]
    head = 0
    for nheads, gain_row, kind in groups:
        c0 = head * HEAD_DIM
        acc = _dot(h, w_ref[:, c0:c0 + nheads * HEAD_DIM])
        for k in range(nheads):
            t = acc[:, k * HEAD_DIM:(k + 1) * HEAD_DIM]
            if gain_row is not None:
                t = _rmsnorm(t, gain_ref[gain_row:gain_row + 1, :])
            if use_rope and kind is not None:
                t = _rotate(t, rope_ref, kind)
            qkv_ref[head + k] = t.astype(BF16)
            if cache_heads is not None:
                for dst_ref, heads in ((ck_ref, cache_heads[0]), (cv_ref, cache_heads[1])):
                    if head + k in heads:
                        slot = heads.index(head + k)
                        for s in range(tm // seq):
                            dst_ref[s, slot] = t[s * seq:(s + 1) * seq, :]
        head += nheads


def _proj_call(x, mods, w_in, gains, rope, li, seq=None, tm=512):
    nb, n, d = x.shape
    tiles = n // tm
    even = li % 2 == 0
    groups = _EVEN_GROUPS if even else _ODD_GROUPS
    use_rope = rope is not None
    cache_heads = None if use_rope else (_EVEN_CACHE if even else _ODD_CACHE)
    in_specs = [
        pl.BlockSpec((1, tm, d), lambda bb, i: (bb, i, 0)),
        pl.BlockSpec((1, N_MOD, d), lambda bb, i: (bb, 0, 0)),
        pl.BlockSpec((None, d, PROJ_W), lambda bb, i: (li, 0, 0), pipeline_mode=pl.Buffered(1)),
        pl.BlockSpec((2, HEAD_DIM), lambda bb, i: (0, 0)),
    ]
    args = [x, mods, w_in, gains]
    if use_rope:
        in_specs.append(pl.BlockSpec((6, tm, HEAD_DIM), lambda bb, i: (0, i, 0)))
        args.append(rope)
    out_shape = [jax.ShapeDtypeStruct((N_PROJ_HEADS, nb * n, HEAD_DIM), BF16)]
    out_specs = [pl.BlockSpec((N_PROJ_HEADS, tm, HEAD_DIM), lambda bb, i: (0, bb * tiles + i, 0))]
    if cache_heads is not None:
        per = tm // seq
        for _ in range(2):
            out_shape.append(jax.ShapeDtypeStruct((nb * n // seq, KV_HEADS, seq, HEAD_DIM), F32))
            out_specs.append(pl.BlockSpec((per, KV_HEADS, seq, HEAD_DIM), lambda bb, i: (bb * tiles + i, 0, 0, 0)))
    return pl.pallas_call(
        functools.partial(_proj_kernel, groups=groups, use_rope=use_rope, cache_heads=cache_heads, seq=seq),
        out_shape=out_shape,
        grid=(nb, tiles),
        in_specs=in_specs,
        out_specs=out_specs,
        compiler_params=_params("parallel", "parallel"),
        name="qkv_proj",
    )(*args)


def _softmax_step(state, s, v):
    row_max = jnp.max(s, axis=-1, keepdims=True)
    if state is None:
        p = jnp.exp(s - row_max)
        return row_max, jnp.sum(p, axis=-1, keepdims=True), _dot(p.astype(BF16), v)
    m, l, acc = state
    m_new = jnp.maximum(m, row_max)
    alpha = jnp.exp(m - m_new)
    p = jnp.exp(s - m_new)
    l = alpha * l + jnp.sum(p, axis=-1, keepdims=True)
    acc = alpha * acc + _dot(p.astype(BF16), v)
    return m_new, l, acc


def _sink_column(sink_ref, first, groups, rows_per_head):
    row = lax.broadcasted_iota(jnp.int32, (groups * rows_per_head, 1), 0)
    col = jnp.full((groups * rows_per_head, 1), sink_ref[first], F32)
    for g in range(1, groups):
        col = jnp.where(row >= g * rows_per_head, sink_ref[first + g], col)
    return col


def _store_heads(o_ref, o, groups, tq):
    for g in range(groups):
        o_ref[:, g * HEAD_DIM:(g + 1) * HEAD_DIM] = o[g * tq:(g + 1) * tq, :].astype(o_ref.dtype)


def _dense_kernel(*refs, groups, tq, nk, tk, has_ctx, has_sink, scale):
    pos = 0
    sink_ref = None
    if has_sink:
        sink_ref = refs[0]
        pos = 1
    q_ref, k_ref, v_ref = refs[pos:pos + 3]
    pos += 3
    if has_ctx:
        kc_ref, vc_ref = refs[pos:pos + 2]
        pos += 2
    o_ref = refs[pos]

    q = q_ref[...].reshape(groups * tq, HEAD_DIM)
    state = None
    for c in range(nk // tk):
        s = _dot_nt(q, k_ref[c * tk:(c + 1) * tk, :]) * scale
        state = _softmax_step(state, s, v_ref[c * tk:(c + 1) * tk, :])
    if has_ctx:
        s = _dot_nt(q, kc_ref[...]) * scale
        state = _softmax_step(state, s, vc_ref[...])
    m, l, acc = state
    if has_sink:
        sink = _sink_column(sink_ref, pl.program_id(1) * groups, groups, tq)
        m_new = jnp.maximum(m, sink)
        alpha = jnp.exp(m - m_new)
        l = alpha * l + jnp.exp(sink - m_new)
        acc = alpha * acc
    _store_heads(o_ref, acc / l, groups, tq)


def _dense_call(qkv, q_head0, k_head0, v_head0, n_kv, groups, nb, n, tq, tk, ctx=None, sink=None):
    qt = n // tq
    in_specs, args = [], []
    if sink is not None:
        in_specs.append(pl.BlockSpec(memory_space=pltpu.SMEM))
        args.append(sink)
    qh0, kh0, vh0 = q_head0 // groups, k_head0, v_head0
    in_specs += [
        pl.BlockSpec((groups, tq, HEAD_DIM), lambda bb, h, i: (qh0 + h, bb * qt + i, 0)),
        pl.BlockSpec((None, n, HEAD_DIM), lambda bb, h, i: (kh0 + h, bb, 0)),
        pl.BlockSpec((None, n, HEAD_DIM), lambda bb, h, i: (vh0 + h, bb, 0)),
    ]
    args += [qkv, qkv, qkv]
    if ctx is not None:
        ck, cv, li, ch0 = ctx
        m = ck.shape[3]
        for arr in (ck, cv):
            in_specs.append(pl.BlockSpec((None, None, None, m, HEAD_DIM), lambda bb, h, i: (bb, li, ch0 + h, 0, 0)))
            args.append(arr)
    kern = functools.partial(_dense_kernel, groups=groups, tq=tq, nk=n, tk=tk, has_ctx=ctx is not None,
                             has_sink=sink is not None, scale=HEAD_DIM ** -0.5)
    return pl.pallas_call(
        kern,
        out_shape=jax.ShapeDtypeStruct((nb * n, n_kv * groups * HEAD_DIM), BF16),
        grid=(nb, n_kv, qt),
        in_specs=in_specs,
        out_specs=pl.BlockSpec((tq, groups * HEAD_DIM), lambda bb, h, i: (bb * qt + i, h)),
        compiler_params=_params("parallel", "parallel", "parallel"),
        name="dense_attn",
    )(*args)


def _diff_kernel(*refs, nk, tk, has_ctx, lam_init):
    q_ref, k_ref, v_ref = refs[:3]
    pos = 3
    if has_ctx:
        kc_ref, vc_ref = refs[pos:pos + 2]
        pos += 2
    lam_ref, gain_ref, o_ref = refs[pos:pos + 3]

    scale = DIFF_HALF ** -0.5
    q = q_ref[...]
    lane = lax.broadcasted_iota(jnp.int32, q.shape, 1)
    zero = jnp.zeros_like(q)
    q1 = jnp.where(lane < DIFF_HALF, q, zero)
    q2 = jnp.where(lane >= DIFF_HALF, q, zero)
    st1 = st2 = None
    for c in range(nk // tk):
        kc = k_ref[c * tk:(c + 1) * tk, :]
        vc = v_ref[c * tk:(c + 1) * tk, :]
        st1 = _softmax_step(st1, _dot_nt(q1, kc) * scale, vc)
        st2 = _softmax_step(st2, _dot_nt(q2, kc) * scale, vc)
    if has_ctx:
        kc = kc_ref[...]
        st1 = _softmax_step(st1, _dot_nt(q1, kc) * scale, vc_ref[...])
        st2 = _softmax_step(st2, _dot_nt(q2, kc) * scale, vc_ref[...])
    lb = lam_ref[...]
    lam = (jnp.exp(jnp.sum(lb[0:1] * lb[1:2], axis=-1, keepdims=True))
           - jnp.exp(jnp.sum(lb[2:3] * lb[3:4], axis=-1, keepdims=True)) + lam_init)
    o = st1[2] / st1[1] - lam * (st2[2] / st2[1])
    o_ref[...] = (_rmsnorm(o, gain_ref[...]) * (1.0 - lam_init)).astype(o_ref.dtype)


def _diff_call(qkv, lam_b, subln, li, nb, n, tq, tk, ctx=None):
    qt = n // tq
    e = li // 2
    lam_init = 0.8 - 0.6 * math.exp(-0.3 * li)
    in_specs = [
        pl.BlockSpec((None, tq, HEAD_DIM), lambda bb, h, i: (12 + h, bb * qt + i, 0)),
        pl.BlockSpec((None, n, HEAD_DIM), lambda bb, h, i: (20 + h, bb, 0)),
        pl.BlockSpec((None, n, HEAD_DIM), lambda bb, h, i: (28 + h, bb, 0)),
    ]
    args = [qkv, qkv, qkv]
    if ctx is not None:
        ck, cv = ctx
        m = ck.shape[3]
        for arr in (ck, cv):
            in_specs.append(pl.BlockSpec((None, None, None, m, HEAD_DIM), lambda bb, h, i: (bb, li, HKV_A + h, 0, 0)))
            args.append(arr)
    in_specs += [
        pl.BlockSpec((None, 4, DIFF_HALF), lambda bb, h, i: (e, 0, 0)),
        pl.BlockSpec((None, 1, HEAD_DIM), lambda bb, h, i: (e, 0, 0)),
    ]
    args += [lam_b, subln.reshape(-1, 1, HEAD_DIM)]
    return pl.pallas_call(
        functools.partial(_diff_kernel, nk=n, tk=tk, has_ctx=ctx is not None, lam_init=lam_init),
        out_shape=jax.ShapeDtypeStruct((nb * n, H_B * HEAD_DIM), BF16),
        grid=(nb, H_B, qt),
        in_specs=in_specs,
        out_specs=pl.BlockSpec((tq, HEAD_DIM), lambda bb, h, i: (bb * qt + i, h)),
        compiler_params=_params("parallel", "parallel", "parallel"),
        name="diff_attn",
    )(*args)


NA_Q_ROWS = 4
NA_K_ROWS = NA_Q_ROWS + NA_ROWS


def _na_window_start(j, rows):
    return jnp.clip(NA_Q_ROWS * j - NA_ROWS // 2, 0, rows - NA_K_ROWS)


def _na_bias(rpb, rows):
    nj = rows // NA_Q_ROWS
    tables = []
    for j in (0, 1, nj - 1):
        ws = int(np.clip(NA_Q_ROWS * j - NA_ROWS // 2, 0, rows - NA_K_ROWS))
        r = (NA_Q_ROWS * j + np.arange(NA_Q_ROWS))[:, None, None, None]
        c = np.arange(GRID_W)[None, :, None, None]
        ar = (ws + np.arange(NA_K_ROWS))[None, None, :, None]
        kc = np.arange(GRID_W)[None, None, None, :]
        rs = np.clip(r - NA_ROWS // 2, 0, rows - NA_ROWS)
        cs = np.clip(c - NA_COLS // 2, 0, GRID_W - NA_COLS)
        valid = (ar >= rs) & (ar < rs + NA_ROWS) & (kc >= cs) & (kc < cs + NA_COLS)
        ro = np.clip(ar - r + NA_ROWS - 1, 0, 2 * NA_ROWS - 2)
        co = np.clip(kc - c + NA_COLS - 1, 0, 2 * NA_COLS - 2)
        shape = (NA_Q_ROWS, GRID_W, NA_K_ROWS, GRID_W)
        ro, co, valid = (np.broadcast_to(a, shape) for a in (ro, co, valid))
        t = jnp.where(valid[None], rpb[:, ro, co].astype(F32), MASK_VALUE)
        tables.append(t.reshape(rpb.shape[0], NA_Q_ROWS * GRID_W, NA_K_ROWS * GRID_W))
    return jnp.stack(tables, axis=1)


def _na_kernel(q_ref, k_ref, v_ref, kc_ref, vc_ref, bias_ref, o_ref, *, rows):
    scale = HEAD_DIM ** -0.5
    start = pl.multiple_of(_na_window_start(pl.program_id(2), rows) * GRID_W, GRID_W)
    span = NA_K_ROWS * GRID_W
    q = q_ref[...]
    s_loc = _dot_nt(q, k_ref[pl.ds(start, span), :]) * scale + bias_ref[...]
    s_ctx = _dot_nt(q, kc_ref[...]) * scale
    m = jnp.maximum(jnp.max(s_loc, axis=-1, keepdims=True), jnp.max(s_ctx, axis=-1, keepdims=True))
    p_loc = jnp.exp(s_loc - m)
    p_ctx = jnp.exp(s_ctx - m)
    l = jnp.sum(p_loc, axis=-1, keepdims=True) + jnp.sum(p_ctx, axis=-1, keepdims=True)
    acc = _dot(p_loc.astype(BF16), v_ref[pl.ds(start, span), :]) + _dot(p_ctx.astype(BF16), vc_ref[...])
    o_ref[...] = (acc / l).astype(o_ref.dtype)


def _na_call(qkv, bias, ck, cv, li, nb, n):
    rows = n // GRID_W
    nj = rows // NA_Q_ROWS
    tq = NA_Q_ROWS * GRID_W
    m = ck.shape[3]

    def bias_map(bb, h, j):
        return (h, jnp.where(j == 0, 0, jnp.where(j == nj - 1, 2, 1)), 0, 0)

    return pl.pallas_call(
        functools.partial(_na_kernel, rows=rows),
        out_shape=jax.ShapeDtypeStruct((nb * n, H_C * HEAD_DIM), BF16),
        grid=(nb, H_C, nj),
        in_specs=[
            pl.BlockSpec((None, tq, HEAD_DIM), lambda bb, h, j: (h, bb * nj + j, 0)),
            pl.BlockSpec((None, n, HEAD_DIM), lambda bb, h, j: (H_C + h, bb, 0)),
            pl.BlockSpec((None, n, HEAD_DIM), lambda bb, h, j: (2 * H_C + h, bb, 0)),
            pl.BlockSpec((None, None, None, m, HEAD_DIM), lambda bb, h, j: (bb, li, h, 0, 0)),
            pl.BlockSpec((None, None, None, m, HEAD_DIM), lambda bb, h, j: (bb, li, h, 0, 0)),
            pl.BlockSpec((None, None, tq, NA_K_ROWS * GRID_W), bias_map),
        ],
        out_specs=pl.BlockSpec((tq, HEAD_DIM), lambda bb, h, j: (bb * nj + j, h)),
        compiler_params=_params("parallel", "parallel", "parallel"),
        name="na2d_attn",
    )(qkv, qkv, qkv, ck, cv, bias)


BAND_SPAN = 3 * WINDOW
BAND_GROUPS = H_D // HKV_D


def _band_kernel(sink_ref, q_ref, k_ref, v_ref, kc_ref, vc_ref, o_ref, *, n):
    scale = HEAD_DIM ** -0.5
    blk = pl.program_id(2)
    rows = BAND_GROUPS * WINDOW
    start = pl.multiple_of(jnp.clip((blk - 1) * WINDOW, 0, n - BAND_SPAN), WINDOW)
    q = q_ref[...].reshape(rows, HEAD_DIM)
    qpos = blk * WINDOW + lax.rem(lax.broadcasted_iota(jnp.int32, (rows, BAND_SPAN), 0), WINDOW)
    kpos = start + lax.broadcasted_iota(jnp.int32, (rows, BAND_SPAN), 1)
    s_loc = _dot_nt(q, k_ref[pl.ds(start, BAND_SPAN), :]) * scale
    s_loc = jnp.where(jnp.abs(qpos - kpos) <= WINDOW, s_loc, MASK_VALUE)
    s_ctx = _dot_nt(q, kc_ref[...]) * scale
    sink = _sink_column(sink_ref, pl.program_id(1) * BAND_GROUPS, BAND_GROUPS, WINDOW)
    m = jnp.maximum(jnp.maximum(jnp.max(s_loc, axis=-1, keepdims=True), jnp.max(s_ctx, axis=-1, keepdims=True)), sink)
    p_loc = jnp.exp(s_loc - m)
    p_ctx = jnp.exp(s_ctx - m)
    l = jnp.sum(p_loc, axis=-1, keepdims=True) + jnp.sum(p_ctx, axis=-1, keepdims=True) + jnp.exp(sink - m)
    acc = _dot(p_loc.astype(BF16), v_ref[pl.ds(start, BAND_SPAN), :]) + _dot(p_ctx.astype(BF16), vc_ref[...])
    _store_heads(o_ref, acc / l, BAND_GROUPS, WINDOW)


def _band_call(qkv, sink, ck, cv, li, nb, n):
    nblk = n // WINDOW
    m = ck.shape[3]
    return pl.pallas_call(
        functools.partial(_band_kernel, n=n),
        out_shape=jax.ShapeDtypeStruct((nb * n, H_D * HEAD_DIM), BF16),
        grid=(nb, HKV_D, nblk),
        in_specs=[
            pl.BlockSpec(memory_space=pltpu.SMEM),
            pl.BlockSpec((BAND_GROUPS, WINDOW, HEAD_DIM), lambda bb, h, i: (24 // BAND_GROUPS + h, bb * nblk + i, 0)),
            pl.BlockSpec((None, n, HEAD_DIM), lambda bb, h, i: (32 + h, bb, 0)),
            pl.BlockSpec((None, n, HEAD_DIM), lambda bb, h, i: (34 + h, bb, 0)),
            pl.BlockSpec((None, None, None, m, HEAD_DIM), lambda bb, h, i: (bb, li, H_C + h, 0, 0)),
            pl.BlockSpec((None, None, None, m, HEAD_DIM), lambda bb, h, i: (bb, li, H_C + h, 0, 0)),
        ],
        out_specs=pl.BlockSpec((WINDOW, BAND_GROUPS * HEAD_DIM), lambda bb, h, i: (bb * nblk + i, h)),
        compiler_params=_params("parallel", "parallel", "parallel"),
        name="band_attn",
    )(sink, qkv, qkv, qkv, ck, cv)


def _outproj_kernel(x_ref, mod_ref, ma_ref, mb_ref, w_ref, g_ref, b_ref, o_ref):
    f = _dot(ma_ref[...], w_ref[:MIX_HALF, :]) + _dot(mb_ref[...], w_ref[MIX_HALF:, :])
    gate = mod_ref[0, 5:6, :]
    o_ref[0] = _layernorm(DN_ALPHA * x_ref[0] + gate * f, g_ref[...], b_ref[...])


def _outproj_call(x, mods, mix_a, mix_b, w_out, ln_g, ln_b, li, tm=512):
    nb, n, d = x.shape
    tiles = n // tm
    return pl.pallas_call(
        _outproj_kernel,
        out_shape=jax.ShapeDtypeStruct(x.shape, F32),
        grid=(nb, tiles),
        in_specs=[
            pl.BlockSpec((1, tm, d), lambda bb, i: (bb, i, 0)),
            pl.BlockSpec((1, N_MOD, d), lambda bb, i: (bb, 0, 0)),
            pl.BlockSpec((tm, MIX_HALF), lambda bb, i: (bb * tiles + i, 0)),
            pl.BlockSpec((tm, MIX_HALF), lambda bb, i: (bb * tiles + i, 0)),
            pl.BlockSpec((None, 2 * MIX_HALF, d), lambda bb, i: (li, 0, 0), pipeline_mode=pl.Buffered(1)),
            pl.BlockSpec((1, d), lambda bb, i: (0, 0)),
            pl.BlockSpec((1, d), lambda bb, i: (0, 0)),
        ],
        out_specs=pl.BlockSpec((1, tm, d), lambda bb, i: (bb, i, 0)),
        compiler_params=_params("parallel", "parallel"),
        name="out_proj",
    )(x, mods, mix_a, mix_b, w_out, ln_g[li, 1][None, :], ln_b[li, 1][None, :])


def _rope_tables(n):
    t = jnp.arange(n, dtype=jnp.int32)
    rowcol = jnp.stack([t // GRID_W, t % GRID_W], axis=-1).astype(F32)
    lane = np.arange(HEAD_DIM)
    out = []
    for half in (32, 16):
        which = (lane // (2 * half)) % 2
        idx = lane % half
        low = (lane % (2 * half)) < half
        freqs = ROPE_THETA ** (-jnp.arange(half, dtype=F32) / half)
        ang = rowcol[:, which] * freqs[idx][None, :]
        cos, sin = jnp.cos(ang), jnp.sin(ang)
        out += [cos, jnp.where(low[None, :], -sin, 0.0), jnp.where(low[None, :], 0.0, sin)]
    return jnp.stack(out, axis=0)


def kernel(x_prompt, x_sample, cache_k, cache_v, c, c_ctx, w_ada, b_ada, ln_g, ln_b, ffn_w1, ffn_w3, ffn_w2,
           w_in, w_out, qk_gain_a, lam_b, subln_b, rpb_c, sink_d):
    batch, seq, d = x_prompt.shape
    dec_batch, dec_seq, _ = x_sample.shape

    w1, w3, w2 = ffn_w1.astype(BF16), ffn_w3.astype(BF16), ffn_w2.astype(BF16)
    w_in_b, w_out_b = w_in.astype(BF16), w_out.astype(BF16)
    ck_b, cv_b = cache_k.astype(BF16), cache_v.astype(BF16)

    cond_rows = 16
    cond = jnp.zeros((cond_rows, d), F32).at[:dec_batch].set(c).at[dec_batch].set(c_ctx)
    mods = _mods_call(cond, w_ada, b_ada).reshape(DEPTH, cond_rows, N_MOD, d)

    rope = _rope_tables(dec_seq)
    rows = dec_seq // GRID_W
    no_gain = jnp.ones((2, HEAD_DIM), F32)

    xp = x_prompt.reshape(1, batch * seq, d)
    xs = x_sample
    new_k, new_v = [], []
    for li in range(DEPTH):
        even = li % 2 == 0
        e = li // 2
        mods_s = mods[li, :dec_batch]
        mods_p = mods[li, dec_batch:dec_batch + 1]
        gains = qk_gain_a[e] if even else no_gain

        xp = _ffn_call(xp, mods_p, w1, w3, w2, ln_g, ln_b, li, 0)
        qkv, ck_new, cv_new = _proj_call(xp, mods_p, w_in_b, gains, None, li, seq=seq)
        new_k.append(ck_new)
        new_v.append(cv_new)
        if even:
            mix_a = _dense_call(qkv, 0, 8, 10, HKV_A, H_A // HKV_A, batch, seq, seq, seq)
            mix_b = _diff_call(qkv, lam_b, subln_b, li, batch, seq, seq, seq)
        else:
            mix_a = _dense_call(qkv, 0, 8, 16, H_C, 1, batch, seq, seq, seq)
            mix_b = _dense_call(qkv, 24, 32, 34, HKV_D, H_D // HKV_D, batch, seq, seq, seq, sink=sink_d[e])
        xp = _outproj_call(xp, mods_p, mix_a, mix_b, w_out_b, ln_g, ln_b, li)
        xp = _ffn_call(xp, mods_p, w1, w3, w2, ln_g, ln_b, li, 1)

        xs = _ffn_call(xs, mods_s, w1, w3, w2, ln_g, ln_b, li, 0)
        (qkv,) = _proj_call(xs, mods_s, w_in_b, gains, rope, li)
        if even:
            mix_a = _dense_call(qkv, 0, 8, 10, HKV_A, H_A // HKV_A, dec_batch, dec_seq, 128, 512,
                                ctx=(ck_b, cv_b, li, 0))
            mix_b = _diff_call(qkv, lam_b, subln_b, li, dec_batch, dec_seq, 512, 512, ctx=(ck_b, cv_b))
        else:
            mix_a = _na_call(qkv, _na_bias(rpb_c[e], rows), ck_b, cv_b, li, dec_batch, dec_seq)
            mix_b = _band_call(qkv, sink_d[e], ck_b, cv_b, li, dec_batch, dec_seq)
        xs = _outproj_call(xs, mods_s, mix_a, mix_b, w_out_b, ln_g, ln_b, li)
        xs = _ffn_call(xs, mods_s, w1, w3, w2, ln_g, ln_b, li, 1)

    y_prompt = xp.reshape(batch, seq, d)
    new_cache_k = jnp.stack(new_k, axis=1)
    new_cache_v = jnp.stack(new_v, axis=1)
    return (y_prompt, xs, new_cache_k, new_cache_v)
```

```python
import functools
import math

import jax
import jax.numpy as jnp
import numpy as np
from jax import lax
from jax.experimental import pallas as pl
from jax.experimental.pallas import tpu as pltpu

D_MODEL = 2048
DEPTH = 4
GRID_W = 64
HEAD_DIM = 128
DIFF_HALF = HEAD_DIM // 2
H_A = 8
HKV_A = 2
H_B = 8
H_C = 8
H_D = 8
HKV_D = 2
KV_HEADS = HKV_A + H_B
D_FF = 5632
NA_ROWS = 8
NA_COLS = 16
WINDOW = 128
ROPE_THETA = 10000.0
LN_EPS = 1e-6
N_MOD = 9
DN_ALPHA = (2 * DEPTH) ** 0.25
PROJ_W = H_A * HEAD_DIM + 2 * HKV_A * HEAD_DIM + 3 * H_B * HEAD_DIM
N_PROJ_HEADS = PROJ_W // HEAD_DIM
MIX_HALF = H_A * HEAD_DIM

V7X_VMEM_BYTES = 64 * 1024 * 1024
VMEM_LIMIT_BYTES = V7X_VMEM_BYTES - 8 * 1024 * 1024
LANES = 128
MASK_VALUE = -1e30
LOG2E = math.log2(math.e)
FFN_COL_CHUNK = 512
OUTPROJ_ROW_SLAB = 256

BF16 = jnp.bfloat16
F32 = jnp.float32


def _params(*semantics):
    return pltpu.CompilerParams(dimension_semantics=semantics, vmem_limit_bytes=VMEM_LIMIT_BYTES)


def _dot(a, b):
    return jnp.dot(a, b, preferred_element_type=F32)


def _dot_nt(a, b):
    return lax.dot_general(a, b, (((1,), (1,)), ((), ())), preferred_element_type=F32)


def _layernorm(y, g, b):
    mu = jnp.mean(y, axis=-1, keepdims=True)
    yc = y - mu
    var = jnp.mean(yc * yc, axis=-1, keepdims=True)
    return yc * lax.rsqrt(var + LN_EPS) * g + b


def _rmsnorm(t, g):
    return t * lax.rsqrt(jnp.mean(t * t, axis=-1, keepdims=True) + LN_EPS) * g


def _mods_kernel(c_ref, w_ref, b_ref, o_ref):
    c = c_ref[...]
    s = (c / (1.0 + jnp.exp(-c))).astype(BF16)
    o_ref[...] = _dot(s, w_ref[...].astype(BF16)) + b_ref[...]


def _mods_call(cond, w_ada, b_ada, tn=1024):
    rows = cond.shape[0]
    width = N_MOD * D_MODEL
    return pl.pallas_call(
        _mods_kernel,
        out_shape=jax.ShapeDtypeStruct((DEPTH, rows, width), F32),
        grid=(DEPTH, width // tn),
        in_specs=[
            pl.BlockSpec((rows, D_MODEL), lambda l, j: (0, 0)),
            pl.BlockSpec((None, D_MODEL, tn), lambda l, j: (l, 0, j)),
            pl.BlockSpec((None, 1, tn), lambda l, j: (l, 0, j)),
        ],
        out_specs=pl.BlockSpec((None, rows, tn), lambda l, j: (l, 0, j)),
        compiler_params=_params("parallel", "parallel"),
        name="adaln_mods",
    )(cond, w_ada, b_ada.reshape(DEPTH, 1, width))


def _ffn_kernel(x_ref, mod_ref, w1_ref, w3_ref, w2_ref, g_ref, b_ref, o_ref, h_ref, *, mod_base):
    j = pl.program_id(2)

    @pl.when(j == 0)
    def _():
        shift = mod_ref[0, mod_base:mod_base + 1, :]
        scale = mod_ref[0, mod_base + 1:mod_base + 2, :]
        h_ref[...] = (x_ref[0] * (1.0 + scale) + shift).astype(BF16)
        o_ref[0] = jnp.zeros(o_ref.shape[1:], F32)

    h = h_ref[...]
    a = _dot(h, w1_ref[...])
    b = _dot(h, w3_ref[...])
    u = (a / (1.0 + jnp.exp(-a)) * b).astype(BF16)
    for c0 in range(0, o_ref.shape[2], FFN_COL_CHUNK):
        o_ref[0, :, c0:c0 + FFN_COL_CHUNK] += _dot(u, w2_ref[:, c0:c0 + FFN_COL_CHUNK])

    @pl.when(j == pl.num_programs(2) - 1)
    def _():
        gate = mod_ref[0, mod_base + 2:mod_base + 3, :]
        y = DN_ALPHA * x_ref[0] + (0.5 * gate) * o_ref[0]
        o_ref[0] = _layernorm(y, g_ref[...], b_ref[...])


def _ffn_call(x, mods, w1, w3, w2, ln_g, ln_b, li, slot, tm=512, tf=512):
    nb, n, d = x.shape
    mod_base = 0 if slot == 0 else 6
    ln_idx = 0 if slot == 0 else 2
    g = ln_g[li, ln_idx][None, :]
    b = ln_b[li, ln_idx][None, :]
    return pl.pallas_call(
        functools.partial(_ffn_kernel, mod_base=mod_base),
        out_shape=jax.ShapeDtypeStruct(x.shape, F32),
        grid=(nb, n // tm, w1.shape[-1] // tf),
        in_specs=[
            pl.BlockSpec((1, tm, d), lambda bb, i, j: (bb, i, 0)),
            pl.BlockSpec((1, N_MOD, d), lambda bb, i, j: (bb, 0, 0)),
            pl.BlockSpec((None, None, d, tf), lambda bb, i, j: (li, slot, 0, j)),
            pl.BlockSpec((None, None, d, tf), lambda bb, i, j: (li, slot, 0, j)),
            pl.BlockSpec((None, None, tf, d), lambda bb, i, j: (li, slot, j, 0)),
            pl.BlockSpec((1, d), lambda bb, i, j: (0, 0)),
            pl.BlockSpec((1, d), lambda bb, i, j: (0, 0)),
        ],
        out_specs=pl.BlockSpec((1, tm, d), lambda bb, i, j: (bb, i, 0)),
        scratch_shapes=[pltpu.VMEM((tm, d), BF16)],
        compiler_params=_params("parallel", "parallel", "arbitrary"),
        name="ffn",
    )(x, mods, w1, w3, w2, g, b)


Q_SCALE = HEAD_DIM ** -0.5 * LOG2E
Q_SCALE_DIFF = DIFF_HALF ** -0.5 * LOG2E
_EVEN_GROUPS = ((H_A, 0, "axial", Q_SCALE), (HKV_A, 1, "axial", None), (HKV_A, None, None, None),
                (H_B, None, "diff", Q_SCALE_DIFF), (H_B, None, "diff", None), (H_B, None, None, None))
_ODD_GROUPS = ((H_C, None, None, Q_SCALE), (H_C, None, None, None), (H_C, None, None, None),
               (H_D, None, "axial", Q_SCALE), (HKV_D, None, "axial", None), (HKV_D, None, None, None))
_EVEN_CACHE = (tuple(range(8, 10)) + tuple(range(20, 28)), tuple(range(10, 12)) + tuple(range(28, 36)))
_ODD_CACHE = (tuple(range(8, 16)) + tuple(range(32, 34)), tuple(range(16, 24)) + tuple(range(34, 36)))


def _rotate(t, rope_ref, kind):
    base, half = (0, 32) if kind == "axial" else (3, 16)
    up = pltpu.roll(t, LANES - half, 1)
    down = pltpu.roll(t, half, 1)
    return t * rope_ref[base] + up * rope_ref[base + 1] + down * rope_ref[base + 2]


def _proj_kernel(*refs, groups, use_rope, cache_heads, seq):
    x_ref, mod_ref, w_ref, gain_ref = refs[:4]
    pos = 4
    rope_ref = None
    if use_rope:
        rope_ref = refs[pos]
        pos += 1
    qkv_ref = refs[pos]
    pos += 1
    ck_ref = cv_ref = None
    if cache_heads is not None:
        ck_ref, cv_ref = refs[pos], refs[pos + 1]

    shift = mod_ref[0, 3:4, :]
    scale = mod_ref[0, 4:5, :]
    h = (x_ref[0] * (1.0 + scale) + shift).astype(BF16)
    tm = h.shape[0]
    head = 0
    for nheads, gain_row, kind, q_scale in groups:
        c0 = head * HEAD_DIM
        acc = _dot(h, w_ref[:, c0:c0 + nheads * HEAD_DIM])
        for k in range(nheads):
            t = acc[:, k * HEAD_DIM:(k + 1) * HEAD_DIM]
            if gain_row is not None:
                t = _rmsnorm(t, gain_ref[gain_row:gain_row + 1, :])
            if use_rope and kind is not None:
                t = _rotate(t, rope_ref, kind)
            qkv_ref[head + k] = (t if q_scale is None else t * q_scale).astype(BF16)
            if cache_heads is not None:
                for dst_ref, heads in ((ck_ref, cache_heads[0]), (cv_ref, cache_heads[1])):
                    if head + k in heads:
                        slot = heads.index(head + k)
                        for s in range(tm // seq):
                            dst_ref[s, slot] = t[s * seq:(s + 1) * seq, :]
        head += nheads


def _proj_call(x, mods, w_in, gains, rope, li, seq=None, tm=512):
    nb, n, d = x.shape
    tiles = n // tm
    even = li % 2 == 0
    groups = _EVEN_GROUPS if even else _ODD_GROUPS
    use_rope = rope is not None
    cache_heads = None if use_rope else (_EVEN_CACHE if even else _ODD_CACHE)
    in_specs = [
        pl.BlockSpec((1, tm, d), lambda bb, i: (bb, i, 0)),
        pl.BlockSpec((1, N_MOD, d), lambda bb, i: (bb, 0, 0)),
        pl.BlockSpec((None, d, PROJ_W), lambda bb, i: (li, 0, 0), pipeline_mode=pl.Buffered(1)),
        pl.BlockSpec((2, HEAD_DIM), lambda bb, i: (0, 0)),
    ]
    args = [x, mods, w_in, gains]
    if use_rope:
        in_specs.append(pl.BlockSpec((6, tm, HEAD_DIM), lambda bb, i: (0, i, 0)))
        args.append(rope)
    out_shape = [jax.ShapeDtypeStruct((N_PROJ_HEADS, nb * n, HEAD_DIM), BF16)]
    out_specs = [pl.BlockSpec((N_PROJ_HEADS, tm, HEAD_DIM), lambda bb, i: (0, bb * tiles + i, 0))]
    if cache_heads is not None:
        per = tm // seq
        for _ in range(2):
            out_shape.append(jax.ShapeDtypeStruct((nb * n // seq, KV_HEADS, seq, HEAD_DIM), F32))
            out_specs.append(pl.BlockSpec((per, KV_HEADS, seq, HEAD_DIM), lambda bb, i: (bb * tiles + i, 0, 0, 0)))
    return pl.pallas_call(
        functools.partial(_proj_kernel, groups=groups, use_rope=use_rope, cache_heads=cache_heads, seq=seq),
        out_shape=out_shape,
        grid=(nb, tiles),
        in_specs=in_specs,
        out_specs=out_specs,
        compiler_params=_params("parallel", "parallel"),
        name="qkv_proj",
    )(*args)


def _softmax_step(state, s, v):
    row_max = jnp.max(s, axis=-1, keepdims=True)
    v1 = _with_ones(v)
    if state is None:
        p = jnp.exp2(s - row_max)
        return row_max, _dot(p.astype(BF16), v1)
    m, acc = state
    m_new = jnp.maximum(m, row_max)
    alpha = jnp.exp2(m - m_new)
    p = jnp.exp2(s - m_new)
    return m_new, alpha * acc + _dot(p.astype(BF16), v1)


def _with_ones(v):
    return jnp.concatenate([v, jnp.ones(v.shape, v.dtype)], axis=1)


def _normalize(acc, extra=None):
    den = acc[:, HEAD_DIM:]
    if extra is not None:
        den = den + extra
    return acc[:, :HEAD_DIM] / den


def _sink_column(sink_ref, first, groups, rows_per_head):
    row = lax.broadcasted_iota(jnp.int32, (groups * rows_per_head, 1), 0)
    col = jnp.full((groups * rows_per_head, 1), sink_ref[first], F32)
    for g in range(1, groups):
        col = jnp.where(row >= g * rows_per_head, sink_ref[first + g], col)
    return col * LOG2E


def _store_heads(o_ref, o, groups, tq):
    for g in range(groups):
        o_ref[:, g * HEAD_DIM:(g + 1) * HEAD_DIM] = o[g * tq:(g + 1) * tq, :].astype(o_ref.dtype)


def _dense_kernel(*refs, groups, tq, nk, tk, has_ctx, has_sink):
    pos = 0
    sink_ref = None
    if has_sink:
        sink_ref = refs[0]
        pos = 1
    q_ref, k_ref, v_ref = refs[pos:pos + 3]
    pos += 3
    if has_ctx:
        kc_ref, vc_ref = refs[pos:pos + 2]
        pos += 2
    o_ref = refs[pos]

    q = q_ref[...].reshape(groups * tq, HEAD_DIM)
    state = None
    for c in range(nk // tk):
        s = _dot_nt(q, k_ref[c * tk:(c + 1) * tk, :])
        state = _softmax_step(state, s, v_ref[c * tk:(c + 1) * tk, :])
    if has_ctx:
        s = _dot_nt(q, kc_ref[...])
        state = _softmax_step(state, s, vc_ref[...])
    m, acc = state
    extra = None
    if has_sink:
        sink = _sink_column(sink_ref, pl.program_id(1) * groups, groups, tq)
        m_new = jnp.maximum(m, sink)
        acc = jnp.exp2(m - m_new) * acc
        extra = jnp.exp2(sink - m_new)
    _store_heads(o_ref, _normalize(acc, extra), groups, tq)


def _dense_call(qkv, q_head0, k_head0, v_head0, n_kv, groups, nb, n, tq, tk, ctx=None, sink=None):
    qt = n // tq
    in_specs, args = [], []
    if sink is not None:
        in_specs.append(pl.BlockSpec(memory_space=pltpu.SMEM))
        args.append(sink)
    qh0, kh0, vh0 = q_head0 // groups, k_head0, v_head0
    in_specs += [
        pl.BlockSpec((groups, tq, HEAD_DIM), lambda bb, h, i: (qh0 + h, bb * qt + i, 0)),
        pl.BlockSpec((None, n, HEAD_DIM), lambda bb, h, i: (kh0 + h, bb, 0)),
        pl.BlockSpec((None, n, HEAD_DIM), lambda bb, h, i: (vh0 + h, bb, 0)),
    ]
    args += [qkv, qkv, qkv]
    if ctx is not None:
        ck, cv, li, ch0 = ctx
        m = ck.shape[3]
        for arr in (ck, cv):
            in_specs.append(pl.BlockSpec((None, None, None, m, HEAD_DIM), lambda bb, h, i: (bb, li, ch0 + h, 0, 0)))
            args.append(arr)
    kern = functools.partial(_dense_kernel, groups=groups, tq=tq, nk=n, tk=tk, has_ctx=ctx is not None,
                             has_sink=sink is not None)
    return pl.pallas_call(
        kern,
        out_shape=jax.ShapeDtypeStruct((nb * n, n_kv * groups * HEAD_DIM), BF16),
        grid=(nb, n_kv, qt),
        in_specs=in_specs,
        out_specs=pl.BlockSpec((tq, groups * HEAD_DIM), lambda bb, h, i: (bb * qt + i, h)),
        compiler_params=_params("parallel", "parallel", "parallel"),
        name="dense_attn",
    )(*args)


def _diff_kernel(*refs, nk, tk, has_ctx, lam_init):
    q_ref, k_ref, v_ref = refs[:3]
    pos = 3
    if has_ctx:
        kc_ref, vc_ref = refs[pos:pos + 2]
        pos += 2
    lam_ref, gain_ref, o_ref = refs[pos:pos + 3]

    q = q_ref[...]
    lane = lax.broadcasted_iota(jnp.int32, q.shape, 1)
    zero = jnp.zeros_like(q)
    q1 = jnp.where(lane < DIFF_HALF, q, zero)
    q2 = jnp.where(lane >= DIFF_HALF, q, zero)
    st1 = st2 = None
    for c in range(nk // tk):
        kc = k_ref[c * tk:(c + 1) * tk, :]
        vc = v_ref[c * tk:(c + 1) * tk, :]
        st1 = _softmax_step(st1, _dot_nt(q1, kc), vc)
        st2 = _softmax_step(st2, _dot_nt(q2, kc), vc)
    if has_ctx:
        kc = kc_ref[...]
        st1 = _softmax_step(st1, _dot_nt(q1, kc), vc_ref[...])
        st2 = _softmax_step(st2, _dot_nt(q2, kc), vc_ref[...])
    lb = lam_ref[...]
    lam = (jnp.exp(jnp.sum(lb[0:1] * lb[1:2], axis=-1, keepdims=True))
           - jnp.exp(jnp.sum(lb[2:3] * lb[3:4], axis=-1, keepdims=True)) + lam_init)
    o = _normalize(st1[1]) - lam * _normalize(st2[1])
    o_ref[...] = (_rmsnorm(o, gain_ref[...]) * (1.0 - lam_init)).astype(o_ref.dtype)


def _diff_call(qkv, lam_b, subln, li, nb, n, tq, tk, ctx=None):
    qt = n // tq
    e = li // 2
    lam_init = 0.8 - 0.6 * math.exp(-0.3 * li)
    in_specs = [
        pl.BlockSpec((None, tq, HEAD_DIM), lambda bb, h, i: (12 + h, bb * qt + i, 0)),
        pl.BlockSpec((None, n, HEAD_DIM), lambda bb, h, i: (20 + h, bb, 0)),
        pl.BlockSpec((None, n, HEAD_DIM), lambda bb, h, i: (28 + h, bb, 0)),
    ]
    args = [qkv, qkv, qkv]
    if ctx is not None:
        ck, cv = ctx
        m = ck.shape[3]
        for arr in (ck, cv):
            in_specs.append(pl.BlockSpec((None, None, None, m, HEAD_DIM), lambda bb, h, i: (bb, li, HKV_A + h, 0, 0)))
            args.append(arr)
    in_specs += [
        pl.BlockSpec((None, 4, DIFF_HALF), lambda bb, h, i: (e, 0, 0)),
        pl.BlockSpec((None, 1, HEAD_DIM), lambda bb, h, i: (e, 0, 0)),
    ]
    args += [lam_b, subln.reshape(-1, 1, HEAD_DIM)]
    return pl.pallas_call(
        functools.partial(_diff_kernel, nk=n, tk=tk, has_ctx=ctx is not None, lam_init=lam_init),
        out_shape=jax.ShapeDtypeStruct((nb * n, H_B * HEAD_DIM), BF16),
        grid=(nb, H_B, qt),
        in_specs=in_specs,
        out_specs=pl.BlockSpec((tq, HEAD_DIM), lambda bb, h, i: (bb * qt + i, h)),
        compiler_params=_params("parallel", "parallel", "parallel"),
        name="diff_attn",
    )(*args)


NA_Q_ROWS = 4
NA_K_ROWS = NA_Q_ROWS + NA_ROWS


def _na_window_start(j, rows):
    return jnp.clip(NA_Q_ROWS * j - NA_ROWS // 2, 0, rows - NA_K_ROWS)


def _na_bias(rpb, rows):
    nj = rows // NA_Q_ROWS
    heads = rpb.shape[0]
    n_ro, n_co = 2 * NA_ROWS - 1, 2 * NA_COLS - 1
    c = np.arange(GRID_W)[:, None]
    kc = np.arange(GRID_W)[None, :]
    cs = np.clip(c - NA_COLS // 2, 0, GRID_W - NA_COLS)
    col_valid = (kc >= cs) & (kc < cs + NA_COLS)
    onehot = (kc - c + NA_COLS - 1)[None] == np.arange(n_co)[:, None, None]
    rpb2 = rpb.astype(F32) * LOG2E
    band = jnp.sum(jnp.where(onehot[None, None], rpb2[:, :, :, None, None], 0.0), axis=2)
    band = jnp.where(col_valid[None, None], band, MASK_VALUE)
    blocks = jnp.concatenate([band, jnp.full((heads, 1, GRID_W, GRID_W), MASK_VALUE, F32)], axis=1)
    pick = np.full((3, NA_Q_ROWS, NA_K_ROWS), n_ro, np.int32)
    for v, j in enumerate((0, 1, nj - 1)):
        ws = int(np.clip(NA_Q_ROWS * j - NA_ROWS // 2, 0, rows - NA_K_ROWS))
        for rq in range(NA_Q_ROWS):
            r = NA_Q_ROWS * j + rq
            rs = int(np.clip(r - NA_ROWS // 2, 0, rows - NA_ROWS))
            for kr in range(NA_K_ROWS):
                if rs <= ws + kr < rs + NA_ROWS:
                    pick[v, rq, kr] = ws + kr - r + NA_ROWS - 1
    rows_of_blocks = []
    for v in range(3):
        for rq in range(NA_Q_ROWS):
            rows_of_blocks.append(jnp.concatenate([blocks[:, int(pick[v, rq, kr])] for kr in range(NA_K_ROWS)], axis=-1))
    table = jnp.stack(rows_of_blocks, axis=1)
    return table.reshape(heads, 3, NA_Q_ROWS * GRID_W, NA_K_ROWS * GRID_W)


def _windowed_softmax(q, k_loc, v_loc, k_ctx, v_ctx, bias=None, mask=None, sink=None):
    s_loc = _dot_nt(q, k_loc)
    if bias is not None:
        s_loc = s_loc + bias
    if mask is not None:
        s_loc = jnp.where(mask, s_loc, MASK_VALUE)
    s_ctx = _dot_nt(q, k_ctx)
    m = jnp.maximum(jnp.max(s_loc, axis=-1, keepdims=True), jnp.max(s_ctx, axis=-1, keepdims=True))
    if sink is not None:
        m = jnp.maximum(m, sink)
    p_loc = jnp.exp2(s_loc - m)
    p_ctx = jnp.exp2(s_ctx - m)
    l = jnp.sum(p_loc, axis=-1, keepdims=True) + jnp.sum(p_ctx, axis=-1, keepdims=True)
    if sink is not None:
        l = l + jnp.exp2(sink - m)
    return (_dot(p_loc.astype(BF16), v_loc) + _dot(p_ctx.astype(BF16), v_ctx)) / l


NA_HEADS_PER_STEP = 4


def _na_kernel(q_ref, k_ref, v_ref, kc_ref, vc_ref, bias_ref, o_ref, *, rows):
    start = pl.multiple_of(_na_window_start(pl.program_id(2), rows) * GRID_W, GRID_W)
    span = NA_K_ROWS * GRID_W
    for h in range(NA_HEADS_PER_STEP):
        o = _windowed_softmax(q_ref[h], k_ref[h, pl.ds(start, span), :], v_ref[h, pl.ds(start, span), :],
                              kc_ref[h], vc_ref[h], bias=bias_ref[h])
        o_ref[:, h * HEAD_DIM:(h + 1) * HEAD_DIM] = o.astype(o_ref.dtype)


def _na_call(qkv, bias, ck, cv, li, nb, n):
    rows = n // GRID_W
    nj = rows // NA_Q_ROWS
    tq = NA_Q_ROWS * GRID_W
    m = ck.shape[3]
    hs = NA_HEADS_PER_STEP
    hblocks = H_C // hs

    def bias_map(bb, h, j):
        return (h, jnp.where(j == 0, 0, jnp.where(j == nj - 1, 2, 1)), 0, 0)

    return pl.pallas_call(
        functools.partial(_na_kernel, rows=rows),
        out_shape=jax.ShapeDtypeStruct((nb * n, H_C * HEAD_DIM), BF16),
        grid=(nb, hblocks, nj),
        in_specs=[
            pl.BlockSpec((hs, tq, HEAD_DIM), lambda bb, h, j: (h, bb * nj + j, 0)),
            pl.BlockSpec((hs, n, HEAD_DIM), lambda bb, h, j: (hblocks + h, bb, 0)),
            pl.BlockSpec((hs, n, HEAD_DIM), lambda bb, h, j: (2 * hblocks + h, bb, 0)),
            pl.BlockSpec((None, None, hs, m, HEAD_DIM), lambda bb, h, j: (bb, li, h, 0, 0)),
            pl.BlockSpec((None, None, hs, m, HEAD_DIM), lambda bb, h, j: (bb, li, h, 0, 0)),
            pl.BlockSpec((hs, None, tq, NA_K_ROWS * GRID_W), bias_map),
        ],
        out_specs=pl.BlockSpec((tq, hs * HEAD_DIM), lambda bb, h, j: (bb * nj + j, h)),
        compiler_params=_params("parallel", "parallel", "parallel"),
        name="na2d_attn",
    )(qkv, qkv, qkv, ck, cv, bias)


BAND_SPAN = 3 * WINDOW
BAND_GROUPS = H_D // HKV_D


def _band_kernel(sink_ref, q_ref, k_ref, v_ref, kc_ref, vc_ref, o_ref, *, n):
    blk = pl.program_id(1)
    rows = BAND_GROUPS * WINDOW
    start = pl.multiple_of(jnp.clip((blk - 1) * WINDOW, 0, n - BAND_SPAN), WINDOW)
    qpos = blk * WINDOW + lax.rem(lax.broadcasted_iota(jnp.int32, (rows, BAND_SPAN), 0), WINDOW)
    kpos = start + lax.broadcasted_iota(jnp.int32, (rows, BAND_SPAN), 1)
    in_band = jnp.abs(qpos - kpos) <= WINDOW
    for h in range(HKV_D):
        q = q_ref[h * BAND_GROUPS:(h + 1) * BAND_GROUPS].reshape(rows, HEAD_DIM)
        sink = _sink_column(sink_ref, h * BAND_GROUPS, BAND_GROUPS, WINDOW)
        o = _windowed_softmax(q, k_ref[h, pl.ds(start, BAND_SPAN), :], v_ref[h, pl.ds(start, BAND_SPAN), :],
                              kc_ref[h], vc_ref[h], mask=in_band, sink=sink)
        for g in range(BAND_GROUPS):
            col = (h * BAND_GROUPS + g) * HEAD_DIM
            o_ref[:, col:col + HEAD_DIM] = o[g * WINDOW:(g + 1) * WINDOW, :].astype(o_ref.dtype)


def _band_call(qkv, sink, ck, cv, li, nb, n):
    nblk = n // WINDOW
    m = ck.shape[3]
    return pl.pallas_call(
        functools.partial(_band_kernel, n=n),
        out_shape=jax.ShapeDtypeStruct((nb * n, H_D * HEAD_DIM), BF16),
        grid=(nb, nblk),
        in_specs=[
            pl.BlockSpec(memory_space=pltpu.SMEM),
            pl.BlockSpec((H_D, WINDOW, HEAD_DIM), lambda bb, i: (24 // H_D, bb * nblk + i, 0)),
            pl.BlockSpec((HKV_D, n, HEAD_DIM), lambda bb, i: (32 // HKV_D, bb, 0)),
            pl.BlockSpec((HKV_D, n, HEAD_DIM), lambda bb, i: (34 // HKV_D, bb, 0)),
            pl.BlockSpec((None, None, HKV_D, m, HEAD_DIM), lambda bb, i: (bb, li, H_C // HKV_D, 0, 0)),
            pl.BlockSpec((None, None, HKV_D, m, HEAD_DIM), lambda bb, i: (bb, li, H_C // HKV_D, 0, 0)),
        ],
        out_specs=pl.BlockSpec((WINDOW, H_D * HEAD_DIM), lambda bb, i: (bb * nblk + i, 0)),
        compiler_params=_params("parallel", "parallel"),
        name="band_attn",
    )(sink, qkv, qkv, qkv, ck, cv)


def _outproj_kernel(x_ref, mod_ref, ma_ref, mb_ref, w_ref, g_ref, b_ref, o_ref):
    gate = mod_ref[0, 5:6, :]
    for r0 in range(0, o_ref.shape[1], OUTPROJ_ROW_SLAB):
        rows = slice(r0, r0 + OUTPROJ_ROW_SLAB)
        f = _dot(ma_ref[rows, :], w_ref[:MIX_HALF, :]) + _dot(mb_ref[rows, :], w_ref[MIX_HALF:, :])
        o_ref[0, rows, :] = _layernorm(DN_ALPHA * x_ref[0, rows, :] + gate * f, g_ref[...], b_ref[...])


def _outproj_call(x, mods, mix_a, mix_b, w_out, ln_g, ln_b, li, tm=512):
    nb, n, d = x.shape
    tiles = n // tm
    return pl.pallas_call(
        _outproj_kernel,
        out_shape=jax.ShapeDtypeStruct(x.shape, F32),
        grid=(nb, tiles),
        in_specs=[
            pl.BlockSpec((1, tm, d), lambda bb, i: (bb, i, 0)),
            pl.BlockSpec((1, N_MOD, d), lambda bb, i: (bb, 0, 0)),
            pl.BlockSpec((tm, MIX_HALF), lambda bb, i: (bb * tiles + i, 0)),
            pl.BlockSpec((tm, MIX_HALF), lambda bb, i: (bb * tiles + i, 0)),
            pl.BlockSpec((None, 2 * MIX_HALF, d), lambda bb, i: (li, 0, 0), pipeline_mode=pl.Buffered(1)),
            pl.BlockSpec((1, d), lambda bb, i: (0, 0)),
            pl.BlockSpec((1, d), lambda bb, i: (0, 0)),
        ],
        out_specs=pl.BlockSpec((1, tm, d), lambda bb, i: (bb, i, 0)),
        compiler_params=_params("parallel", "parallel"),
        name="out_proj",
    )(x, mods, mix_a, mix_b, w_out, ln_g[li, 1][None, :], ln_b[li, 1][None, :])


def _rope_tables(n):
    t = jnp.arange(n, dtype=jnp.int32)
    rowcol = jnp.stack([t // GRID_W, t % GRID_W], axis=-1).astype(F32)
    lane = np.arange(HEAD_DIM)
    out = []
    for half in (32, 16):
        which = (lane // (2 * half)) % 2
        idx = lane % half
        low = (lane % (2 * half)) < half
        freqs = ROPE_THETA ** (-jnp.arange(half, dtype=F32) / half)
        ang = rowcol[:, which] * freqs[idx][None, :]
        cos, sin = jnp.cos(ang), jnp.sin(ang)
        out += [cos, jnp.where(low[None, :], -sin, 0.0), jnp.where(low[None, :], 0.0, sin)]
    return jnp.stack(out, axis=0)


def kernel(x_prompt, x_sample, cache_k, cache_v, c, c_ctx, w_ada, b_ada, ln_g, ln_b, ffn_w1, ffn_w3, ffn_w2,
           w_in, w_out, qk_gain_a, lam_b, subln_b, rpb_c, sink_d):
    batch, seq, d = x_prompt.shape
    dec_batch, dec_seq, _ = x_sample.shape

    w1, w3, w2 = ffn_w1.astype(BF16), ffn_w3.astype(BF16), ffn_w2.astype(BF16)
    w_in_b, w_out_b = w_in.astype(BF16), w_out.astype(BF16)
    ck_b, cv_b = cache_k.astype(BF16), cache_v.astype(BF16)

    cond_rows = 16
    cond = jnp.zeros((cond_rows, d), F32).at[:dec_batch].set(c).at[dec_batch].set(c_ctx)
    mods = _mods_call(cond, w_ada, b_ada).reshape(DEPTH, cond_rows, N_MOD, d)

    rope = _rope_tables(dec_seq)
    rows = dec_seq // GRID_W
    no_gain = jnp.ones((2, HEAD_DIM), F32)

    xp = x_prompt.reshape(1, batch * seq, d)
    xs = x_sample
    new_k, new_v = [], []
    for li in range(DEPTH):
        even = li % 2 == 0
        e = li // 2
        mods_s = mods[li, :dec_batch]
        mods_p = mods[li, dec_batch:dec_batch + 1]
        gains = qk_gain_a[e] if even else no_gain

        xp = _ffn_call(xp, mods_p, w1, w3, w2, ln_g, ln_b, li, 0)
        qkv, ck_new, cv_new = _proj_call(xp, mods_p, w_in_b, gains, None, li, seq=seq)
        new_k.append(ck_new)
        new_v.append(cv_new)
        if even:
            mix_a = _dense_call(qkv, 0, 8, 10, HKV_A, H_A // HKV_A, batch, seq, seq, seq)
            mix_b = _diff_call(qkv, lam_b, subln_b, li, batch, seq, seq, seq)
        else:
            mix_a = _dense_call(qkv, 0, 8, 16, H_C, 1, batch, seq, seq, seq)
            mix_b = _dense_call(qkv, 24, 32, 34, HKV_D, H_D // HKV_D, batch, seq, seq, seq, sink=sink_d[e])
        xp = _outproj_call(xp, mods_p, mix_a, mix_b, w_out_b, ln_g, ln_b, li)
        xp = _ffn_call(xp, mods_p, w1, w3, w2, ln_g, ln_b, li, 1)

        xs = _ffn_call(xs, mods_s, w1, w3, w2, ln_g, ln_b, li, 0)
        (qkv,) = _proj_call(xs, mods_s, w_in_b, gains, rope, li)
        if even:
            mix_a = _dense_call(qkv, 0, 8, 10, HKV_A, H_A // HKV_A, dec_batch, dec_seq, 256, 512,
                                ctx=(ck_b, cv_b, li, 0))
            mix_b = _diff_call(qkv, lam_b, subln_b, li, dec_batch, dec_seq, 1024, 512, ctx=(ck_b, cv_b))
        else:
            mix_a = _na_call(qkv, _na_bias(rpb_c[e], rows), ck_b, cv_b, li, dec_batch, dec_seq)
            mix_b = _band_call(qkv, sink_d[e], ck_b, cv_b, li, dec_batch, dec_seq)
        xs = _outproj_call(xs, mods_s, mix_a, mix_b, w_out_b, ln_g, ln_b, li)
        xs = _ffn_call(xs, mods_s, w1, w3, w2, ln_g, ln_b, li, 1)

    y_prompt = xp.reshape(batch, seq, d)
    new_cache_k = jnp.stack(new_k, axis=1)
    new_cache_v = jnp.stack(new_v, axis=1)
    return (y_prompt, xs, new_cache_k, new_cache_v)
```

```python
import functools
import math

import jax
import jax.numpy as jnp
import numpy as np
from jax import lax
from jax.experimental import pallas as pl
from jax.experimental.pallas import tpu as pltpu

D_MODEL = 2048
DEPTH = 4
GRID_W = 64
HEAD_DIM = 128
DIFF_HALF = HEAD_DIM // 2
H_A = 8
HKV_A = 2
H_B = 8
H_C = 8
H_D = 8
HKV_D = 2
KV_HEADS = HKV_A + H_B
D_FF = 5632
NA_ROWS = 8
NA_COLS = 16
WINDOW = 128
ROPE_THETA = 10000.0
LN_EPS = 1e-6
N_MOD = 9
DN_ALPHA = (2 * DEPTH) ** 0.25
PROJ_W = H_A * HEAD_DIM + 2 * HKV_A * HEAD_DIM + 3 * H_B * HEAD_DIM
N_PROJ_HEADS = PROJ_W // HEAD_DIM
MIX_HALF = H_A * HEAD_DIM

V7X_VMEM_BYTES = 64 * 1024 * 1024
VMEM_LIMIT_BYTES = V7X_VMEM_BYTES - 8 * 1024 * 1024
FFN_VMEM_LIMIT_BYTES = V7X_VMEM_BYTES - 3 * 1024 * 1024
LANES = 128
MASK_VALUE = -1e30
LOG2E = math.log2(math.e)
FFN_COL_CHUNK = 512
FFN_ROW_SLAB = 512
OUTPROJ_ROW_SLAB = 256

BF16 = jnp.bfloat16
F32 = jnp.float32


def _params(*semantics, vmem_limit=VMEM_LIMIT_BYTES):
    return pltpu.CompilerParams(dimension_semantics=semantics, vmem_limit_bytes=vmem_limit)


def _dot(a, b):
    return jnp.dot(a, b, preferred_element_type=F32)


def _dot_nt(a, b):
    return lax.dot_general(a, b, (((1,), (1,)), ((), ())), preferred_element_type=F32)


def _layernorm(y, g, b):
    mu = jnp.mean(y, axis=-1, keepdims=True)
    yc = y - mu
    var = jnp.mean(yc * yc, axis=-1, keepdims=True)
    return yc * lax.rsqrt(var + LN_EPS) * g + b


def _rmsnorm(t, g):
    return t * lax.rsqrt(jnp.mean(t * t, axis=-1, keepdims=True) + LN_EPS) * g


def _mods_kernel(c_ref, w_ref, b_ref, o_ref):
    c = c_ref[...]
    s = (c / (1.0 + jnp.exp(-c))).astype(BF16)
    o_ref[...] = _dot(s, w_ref[...].astype(BF16)) + b_ref[...]


def _mods_call(cond, w_ada, b_ada, tn=1024):
    rows = cond.shape[0]
    width = N_MOD * D_MODEL
    return pl.pallas_call(
        _mods_kernel,
        out_shape=jax.ShapeDtypeStruct((DEPTH, rows, width), F32),
        grid=(DEPTH, width // tn),
        in_specs=[
            pl.BlockSpec((rows, D_MODEL), lambda l, j: (0, 0)),
            pl.BlockSpec((None, D_MODEL, tn), lambda l, j: (l, 0, j)),
            pl.BlockSpec((None, 1, tn), lambda l, j: (l, 0, j)),
        ],
        out_specs=pl.BlockSpec((None, rows, tn), lambda l, j: (l, 0, j)),
        compiler_params=_params("parallel", "parallel"),
        name="adaln_mods",
    )(cond, w_ada, b_ada.reshape(DEPTH, 1, width))


def _ffn_kernel(x_ref, mod_ref, w1_ref, w3_ref, w2_ref, g_ref, b_ref, o_ref, h_ref, *, mod_base):
    j = pl.program_id(2)
    last = pl.num_programs(2) - 1

    def step(first, final):
        for r0 in range(0, o_ref.shape[1], FFN_ROW_SLAB):
            rows = slice(r0, r0 + FFN_ROW_SLAB)
            if first:
                shift = mod_ref[0, mod_base:mod_base + 1, :]
                scale = mod_ref[0, mod_base + 1:mod_base + 2, :]
                h = (x_ref[0, rows, :] * (1.0 + scale) + shift).astype(BF16)
                h_ref[rows, :] = h
            else:
                h = h_ref[rows, :]
            a = _dot(h, w1_ref[...])
            b = _dot(h, w3_ref[...])
            u = (a / (1.0 + jnp.exp(-a)) * b).astype(BF16)
            for c0 in range(0, o_ref.shape[2], FFN_COL_CHUNK):
                cols = slice(c0, c0 + FFN_COL_CHUNK)
                part = _dot(u, w2_ref[:, cols])
                if first:
                    o_ref[0, rows, cols] = part
                else:
                    o_ref[0, rows, cols] += part
            if final:
                gate = mod_ref[0, mod_base + 2:mod_base + 3, :]
                y = DN_ALPHA * x_ref[0, rows, :] + (0.5 * gate) * o_ref[0, rows, :]
                o_ref[0, rows, :] = _layernorm(y, g_ref[...], b_ref[...])

    pl.when(j == 0)(functools.partial(step, True, False))
    pl.when(jnp.logical_and(j > 0, j < last))(functools.partial(step, False, False))
    pl.when(j == last)(functools.partial(step, False, True))


def _ffn_call(x, mods, w1, w3, w2, ln_g, ln_b, li, slot, tm=1024, tf=512):
    nb, n, d = x.shape
    tm = min(tm, n)
    assert n % tm == 0 and tm % FFN_ROW_SLAB == 0 and w1.shape[-1] // tf >= 2
    mod_base = 0 if slot == 0 else 6
    ln_idx = 0 if slot == 0 else 2
    g = ln_g[li, ln_idx][None, :]
    b = ln_b[li, ln_idx][None, :]
    return pl.pallas_call(
        functools.partial(_ffn_kernel, mod_base=mod_base),
        out_shape=jax.ShapeDtypeStruct(x.shape, F32),
        grid=(nb, n // tm, w1.shape[-1] // tf),
        in_specs=[
            pl.BlockSpec((1, tm, d), lambda bb, i, j: (bb, i, 0)),
            pl.BlockSpec((1, N_MOD, d), lambda bb, i, j: (bb, 0, 0)),
            pl.BlockSpec((None, None, d, tf), lambda bb, i, j: (li, slot, 0, j)),
            pl.BlockSpec((None, None, d, tf), lambda bb, i, j: (li, slot, 0, j)),
            pl.BlockSpec((None, None, tf, d), lambda bb, i, j: (li, slot, j, 0)),
            pl.BlockSpec((1, d), lambda bb, i, j: (0, 0)),
            pl.BlockSpec((1, d), lambda bb, i, j: (0, 0)),
        ],
        out_specs=pl.BlockSpec((1, tm, d), lambda bb, i, j: (bb, i, 0)),
        scratch_shapes=[pltpu.VMEM((tm, d), BF16)],
        compiler_params=_params("parallel", "parallel", "arbitrary", vmem_limit=FFN_VMEM_LIMIT_BYTES),
        name="ffn",
    )(x, mods, w1, w3, w2, g, b)


Q_SCALE = HEAD_DIM ** -0.5 * LOG2E
Q_SCALE_DIFF = DIFF_HALF ** -0.5 * LOG2E
_EVEN_GROUPS = ((H_A, 0, "axial", Q_SCALE), (HKV_A, 1, "axial", None), (HKV_A, None, None, None),
                (H_B, None, "diff", Q_SCALE_DIFF), (H_B, None, "diff", None), (H_B, None, None, None))
_ODD_GROUPS = ((H_C, None, None, Q_SCALE), (H_C, None, None, None), (H_C, None, None, None),
               (H_D, None, "axial", Q_SCALE), (HKV_D, None, "axial", None), (HKV_D, None, None, None))
_EVEN_CACHE = (tuple(range(8, 10)) + tuple(range(20, 28)), tuple(range(10, 12)) + tuple(range(28, 36)))
_ODD_CACHE = (tuple(range(8, 16)) + tuple(range(32, 34)), tuple(range(16, 24)) + tuple(range(34, 36)))


def _rotate(t, rope_ref, kind):
    base, half = (0, 32) if kind == "axial" else (3, 16)
    up = pltpu.roll(t, LANES - half, 1)
    down = pltpu.roll(t, half, 1)
    return t * rope_ref[base] + up * rope_ref[base + 1] + down * rope_ref[base + 2]


def _proj_kernel(*refs, groups, use_rope, cache_heads, seq):
    x_ref, mod_ref, w_ref, gain_ref = refs[:4]
    pos = 4
    rope_ref = None
    if use_rope:
        rope_ref = refs[pos]
        pos += 1
    qkv_ref = refs[pos]
    pos += 1
    ck_ref = cv_ref = None
    if cache_heads is not None:
        ck_ref, cv_ref = refs[pos], refs[pos + 1]

    shift = mod_ref[0, 3:4, :]
    scale = mod_ref[0, 4:5, :]
    h = (x_ref[0] * (1.0 + scale) + shift).astype(BF16)
    tm = h.shape[0]
    head = 0
    for nheads, gain_row, kind, q_scale in groups:
        c0 = head * HEAD_DIM
        acc = _dot(h, w_ref[:, c0:c0 + nheads * HEAD_DIM])
        for k in range(nheads):
            t = acc[:, k * HEAD_DIM:(k + 1) * HEAD_DIM]
            if gain_row is not None:
                t = _rmsnorm(t, gain_ref[gain_row:gain_row + 1, :])
            if use_rope and kind is not None:
                t = _rotate(t, rope_ref, kind)
            qkv_ref[head + k] = (t if q_scale is None else t * q_scale).astype(BF16)
            if cache_heads is not None:
                for dst_ref, heads in ((ck_ref, cache_heads[0]), (cv_ref, cache_heads[1])):
                    if head + k in heads:
                        slot = heads.index(head + k)
                        for s in range(tm // seq):
                            dst_ref[s, slot] = t[s * seq:(s + 1) * seq, :]
        head += nheads


def _proj_call(x, mods, w_in, gains, rope, li, seq=None, tm=512):
    nb, n, d = x.shape
    assert n % tm == 0 and (seq is None or tm % seq == 0)
    tiles = n // tm
    even = li % 2 == 0
    groups = _EVEN_GROUPS if even else _ODD_GROUPS
    use_rope = rope is not None
    cache_heads = None if use_rope else (_EVEN_CACHE if even else _ODD_CACHE)
    in_specs = [
        pl.BlockSpec((1, tm, d), lambda bb, i: (bb, i, 0)),
        pl.BlockSpec((1, N_MOD, d), lambda bb, i: (bb, 0, 0)),
        pl.BlockSpec((None, d, PROJ_W), lambda bb, i: (li, 0, 0), pipeline_mode=pl.Buffered(1)),
        pl.BlockSpec((2, HEAD_DIM), lambda bb, i: (0, 0)),
    ]
    args = [x, mods, w_in, gains]
    if use_rope:
        in_specs.append(pl.BlockSpec((6, tm, HEAD_DIM), lambda bb, i: (0, i, 0)))
        args.append(rope)
    out_shape = [jax.ShapeDtypeStruct((N_PROJ_HEADS, nb * n, HEAD_DIM), BF16)]
    out_specs = [pl.BlockSpec((N_PROJ_HEADS, tm, HEAD_DIM), lambda bb, i: (0, bb * tiles + i, 0))]
    if cache_heads is not None:
        per = tm // seq
        for _ in range(2):
            out_shape.append(jax.ShapeDtypeStruct((nb * n // seq, KV_HEADS, seq, HEAD_DIM), F32))
            out_specs.append(pl.BlockSpec((per, KV_HEADS, seq, HEAD_DIM), lambda bb, i: (bb * tiles + i, 0, 0, 0)))
    return pl.pallas_call(
        functools.partial(_proj_kernel, groups=groups, use_rope=use_rope, cache_heads=cache_heads, seq=seq),
        out_shape=out_shape,
        grid=(nb, tiles),
        in_specs=in_specs,
        out_specs=out_specs,
        compiler_params=_params("parallel", "parallel"),
        name="qkv_proj",
    )(*args)


def _softmax_step(state, s, v):
    row_max = jnp.max(s, axis=-1, keepdims=True)
    v1 = _with_ones(v)
    if state is None:
        p = jnp.exp2(s - row_max)
        return row_max, _dot(p.astype(BF16), v1)
    m, acc = state
    m_new = jnp.maximum(m, row_max)
    alpha = jnp.exp2(m - m_new)
    p = jnp.exp2(s - m_new)
    return m_new, alpha * acc + _dot(p.astype(BF16), v1)


def _with_ones(v):
    return jnp.concatenate([v, jnp.ones(v.shape, v.dtype)], axis=1)


def _normalize(acc, extra=None):
    den = acc[:, HEAD_DIM:]
    if extra is not None:
        den = den + extra
    return acc[:, :HEAD_DIM] / den


def _sink_column(sink_ref, first, groups, rows_per_head):
    row = lax.broadcasted_iota(jnp.int32, (groups * rows_per_head, 1), 0)
    col = jnp.full((groups * rows_per_head, 1), sink_ref[first], F32)
    for g in range(1, groups):
        col = jnp.where(row >= g * rows_per_head, sink_ref[first + g], col)
    return col * LOG2E


def _store_heads(o_ref, o, groups, tq):
    for g in range(groups):
        o_ref[:, g * HEAD_DIM:(g + 1) * HEAD_DIM] = o[g * tq:(g + 1) * tq, :].astype(o_ref.dtype)


def _dense_kernel(*refs, groups, tq, nk, tk, has_ctx, has_sink):
    pos = 0
    sink_ref = None
    if has_sink:
        sink_ref = refs[0]
        pos = 1
    q_ref, k_ref, v_ref = refs[pos:pos + 3]
    pos += 3
    if has_ctx:
        kc_ref, vc_ref = refs[pos:pos + 2]
        pos += 2
    o_ref = refs[pos]

    q = q_ref[...].reshape(groups * tq, HEAD_DIM)
    state = None
    for c in range(nk // tk):
        s = _dot_nt(q, k_ref[c * tk:(c + 1) * tk, :])
        state = _softmax_step(state, s, v_ref[c * tk:(c + 1) * tk, :])
    if has_ctx:
        s = _dot_nt(q, kc_ref[...])
        state = _softmax_step(state, s, vc_ref[...])
    m, acc = state
    extra = None
    if has_sink:
        sink = _sink_column(sink_ref, pl.program_id(1) * groups, groups, tq)
        m_new = jnp.maximum(m, sink)
        acc = jnp.exp2(m - m_new) * acc
        extra = jnp.exp2(sink - m_new)
    _store_heads(o_ref, _normalize(acc, extra), groups, tq)


def _dense_call(qkv, q_head0, k_head0, v_head0, n_kv, groups, nb, n, tq, tk, ctx=None, sink=None):
    qt = n // tq
    in_specs, args = [], []
    if sink is not None:
        in_specs.append(pl.BlockSpec(memory_space=pltpu.SMEM))
        args.append(sink)
    qh0, kh0, vh0 = q_head0 // groups, k_head0, v_head0
    in_specs += [
        pl.BlockSpec((groups, tq, HEAD_DIM), lambda bb, h, i: (qh0 + h, bb * qt + i, 0)),
        pl.BlockSpec((None, n, HEAD_DIM), lambda bb, h, i: (kh0 + h, bb, 0)),
        pl.BlockSpec((None, n, HEAD_DIM), lambda bb, h, i: (vh0 + h, bb, 0)),
    ]
    args += [qkv, qkv, qkv]
    if ctx is not None:
        ck, cv, li, ch0 = ctx
        m = ck.shape[3]
        for arr in (ck, cv):
            in_specs.append(pl.BlockSpec((None, None, None, m, HEAD_DIM), lambda bb, h, i: (bb, li, ch0 + h, 0, 0)))
            args.append(arr)
    kern = functools.partial(_dense_kernel, groups=groups, tq=tq, nk=n, tk=tk, has_ctx=ctx is not None,
                             has_sink=sink is not None)
    return pl.pallas_call(
        kern,
        out_shape=jax.ShapeDtypeStruct((nb * n, n_kv * groups * HEAD_DIM), BF16),
        grid=(nb, n_kv, qt),
        in_specs=in_specs,
        out_specs=pl.BlockSpec((tq, groups * HEAD_DIM), lambda bb, h, i: (bb * qt + i, h)),
        compiler_params=_params("parallel", "parallel", "parallel"),
        name="dense_attn",
    )(*args)


def _diff_kernel(*refs, nk, tk, has_ctx, lam_init):
    q_ref, k_ref, v_ref = refs[:3]
    pos = 3
    if has_ctx:
        kc_ref, vc_ref = refs[pos:pos + 2]
        pos += 2
    lam_ref, gain_ref, o_ref = refs[pos:pos + 3]

    q = q_ref[...]
    lane = lax.broadcasted_iota(jnp.int32, q.shape, 1)
    zero = jnp.zeros_like(q)
    q1 = jnp.where(lane < DIFF_HALF, q, zero)
    q2 = jnp.where(lane >= DIFF_HALF, q, zero)
    st1 = st2 = None
    for c in range(nk // tk):
        kc = k_ref[c * tk:(c + 1) * tk, :]
        vc = v_ref[c * tk:(c + 1) * tk, :]
        st1 = _softmax_step(st1, _dot_nt(q1, kc), vc)
        st2 = _softmax_step(st2, _dot_nt(q2, kc), vc)
    if has_ctx:
        kc = kc_ref[...]
        st1 = _softmax_step(st1, _dot_nt(q1, kc), vc_ref[...])
        st2 = _softmax_step(st2, _dot_nt(q2, kc), vc_ref[...])
    lb = lam_ref[...]
    lam = (jnp.exp(jnp.sum(lb[0:1] * lb[1:2], axis=-1, keepdims=True))
           - jnp.exp(jnp.sum(lb[2:3] * lb[3:4], axis=-1, keepdims=True)) + lam_init)
    o = _normalize(st1[1]) - lam * _normalize(st2[1])
    o_ref[...] = (_rmsnorm(o, gain_ref[...]) * (1.0 - lam_init)).astype(o_ref.dtype)


def _diff_call(qkv, lam_b, subln, li, nb, n, tq, tk, ctx=None):
    qt = n // tq
    e = li // 2
    lam_init = 0.8 - 0.6 * math.exp(-0.3 * li)
    in_specs = [
        pl.BlockSpec((None, tq, HEAD_DIM), lambda bb, h, i: (12 + h, bb * qt + i, 0)),
        pl.BlockSpec((None, n, HEAD_DIM), lambda bb, h, i: (20 + h, bb, 0)),
        pl.BlockSpec((None, n, HEAD_DIM), lambda bb, h, i: (28 + h, bb, 0)),
    ]
    args = [qkv, qkv, qkv]
    if ctx is not None:
        ck, cv = ctx
        m = ck.shape[3]
        for arr in (ck, cv):
            in_specs.append(pl.BlockSpec((None, None, None, m, HEAD_DIM), lambda bb, h, i: (bb, li, HKV_A + h, 0, 0)))
            args.append(arr)
    in_specs += [
        pl.BlockSpec((None, 4, DIFF_HALF), lambda bb, h, i: (e, 0, 0)),
        pl.BlockSpec((None, 1, HEAD_DIM), lambda bb, h, i: (e, 0, 0)),
    ]
    args += [lam_b, subln.reshape(-1, 1, HEAD_DIM)]
    return pl.pallas_call(
        functools.partial(_diff_kernel, nk=n, tk=tk, has_ctx=ctx is not None, lam_init=lam_init),
        out_shape=jax.ShapeDtypeStruct((nb * n, H_B * HEAD_DIM), BF16),
        grid=(nb, H_B, qt),
        in_specs=in_specs,
        out_specs=pl.BlockSpec((tq, HEAD_DIM), lambda bb, h, i: (bb * qt + i, h)),
        compiler_params=_params("parallel", "parallel", "parallel"),
        name="diff_attn",
    )(*args)


NA_Q_ROWS = 4
NA_K_ROWS = NA_Q_ROWS + NA_ROWS


def _na_window_start(j, rows):
    return jnp.clip(NA_Q_ROWS * j - NA_ROWS // 2, 0, rows - NA_K_ROWS)


def _na_bias(rpb, rows):
    nj = rows // NA_Q_ROWS
    heads = rpb.shape[0]
    n_ro, n_co = 2 * NA_ROWS - 1, 2 * NA_COLS - 1
    c = np.arange(GRID_W)[:, None]
    kc = np.arange(GRID_W)[None, :]
    cs = np.clip(c - NA_COLS // 2, 0, GRID_W - NA_COLS)
    col_valid = (kc >= cs) & (kc < cs + NA_COLS)
    onehot = (kc - c + NA_COLS - 1)[None] == np.arange(n_co)[:, None, None]
    rpb2 = rpb.astype(F32) * LOG2E
    band = jnp.sum(jnp.where(onehot[None, None], rpb2[:, :, :, None, None], 0.0), axis=2)
    band = jnp.where(col_valid[None, None], band, MASK_VALUE)
    blocks = jnp.concatenate([band, jnp.full((heads, 1, GRID_W, GRID_W), MASK_VALUE, F32)], axis=1)
    pick = np.full((3, NA_Q_ROWS, NA_K_ROWS), n_ro, np.int32)
    for v, j in enumerate((0, 1, nj - 1)):
        ws = int(np.clip(NA_Q_ROWS * j - NA_ROWS // 2, 0, rows - NA_K_ROWS))
        for rq in range(NA_Q_ROWS):
            r = NA_Q_ROWS * j + rq
            rs = int(np.clip(r - NA_ROWS // 2, 0, rows - NA_ROWS))
            for kr in range(NA_K_ROWS):
                if rs <= ws + kr < rs + NA_ROWS:
                    pick[v, rq, kr] = ws + kr - r + NA_ROWS - 1
    rows_of_blocks = []
    for v in range(3):
        for rq in range(NA_Q_ROWS):
            rows_of_blocks.append(jnp.concatenate([blocks[:, int(pick[v, rq, kr])] for kr in range(NA_K_ROWS)], axis=-1))
    table = jnp.stack(rows_of_blocks, axis=1)
    return table.reshape(heads, 3, NA_Q_ROWS * GRID_W, NA_K_ROWS * GRID_W)


def _windowed_softmax(q, k_loc, v_loc, k_ctx, v_ctx, bias=None, mask=None, sink=None):
    s_loc = _dot_nt(q, k_loc)
    if bias is not None:
        s_loc = s_loc + bias
    if mask is not None:
        s_loc = jnp.where(mask, s_loc, MASK_VALUE)
    s_ctx = _dot_nt(q, k_ctx)
    m = jnp.maximum(jnp.max(s_loc, axis=-1, keepdims=True), jnp.max(s_ctx, axis=-1, keepdims=True))
    if sink is not None:
        m = jnp.maximum(m, sink)
    p_loc = jnp.exp2(s_loc - m)
    p_ctx = jnp.exp2(s_ctx - m)
    l = jnp.sum(p_loc, axis=-1, keepdims=True) + jnp.sum(p_ctx, axis=-1, keepdims=True)
    if sink is not None:
        l = l + jnp.exp2(sink - m)
    return (_dot(p_loc.astype(BF16), v_loc) + _dot(p_ctx.astype(BF16), v_ctx)) / l


NA_HEADS_PER_STEP = 4


def _na_kernel(q_ref, k_ref, v_ref, kc_ref, vc_ref, bias_ref, o_ref, *, rows):
    start = pl.multiple_of(_na_window_start(pl.program_id(2), rows) * GRID_W, GRID_W)
    span = NA_K_ROWS * GRID_W
    for h in range(NA_HEADS_PER_STEP):
        o = _windowed_softmax(q_ref[h], k_ref[h, pl.ds(start, span), :], v_ref[h, pl.ds(start, span), :],
                              kc_ref[h], vc_ref[h], bias=bias_ref[h])
        o_ref[:, h * HEAD_DIM:(h + 1) * HEAD_DIM] = o.astype(o_ref.dtype)


def _na_call(qkv, bias, ck, cv, li, nb, n):
    rows = n // GRID_W
    nj = rows // NA_Q_ROWS
    tq = NA_Q_ROWS * GRID_W
    m = ck.shape[3]
    hs = NA_HEADS_PER_STEP
    hblocks = H_C // hs

    def bias_map(bb, h, j):
        return (h, jnp.where(j == 0, 0, jnp.where(j == nj - 1, 2, 1)), 0, 0)

    return pl.pallas_call(
        functools.partial(_na_kernel, rows=rows),
        out_shape=jax.ShapeDtypeStruct((nb * n, H_C * HEAD_DIM), BF16),
        grid=(nb, hblocks, nj),
        in_specs=[
            pl.BlockSpec((hs, tq, HEAD_DIM), lambda bb, h, j: (h, bb * nj + j, 0)),
            pl.BlockSpec((hs, n, HEAD_DIM), lambda bb, h, j: (hblocks + h, bb, 0)),
            pl.BlockSpec((hs, n, HEAD_DIM), lambda bb, h, j: (2 * hblocks + h, bb, 0)),
            pl.BlockSpec((None, None, hs, m, HEAD_DIM), lambda bb, h, j: (bb, li, h, 0, 0)),
            pl.BlockSpec((None, None, hs, m, HEAD_DIM), lambda bb, h, j: (bb, li, h, 0, 0)),
            pl.BlockSpec((hs, None, tq, NA_K_ROWS * GRID_W), bias_map),
        ],
        out_specs=pl.BlockSpec((tq, hs * HEAD_DIM), lambda bb, h, j: (bb * nj + j, h)),
        compiler_params=_params("parallel", "parallel", "parallel"),
        name="na2d_attn",
    )(qkv, qkv, qkv, ck, cv, bias)


BAND_SPAN = 3 * WINDOW
BAND_GROUPS = H_D // HKV_D


def _band_kernel(sink_ref, q_ref, k_ref, v_ref, kc_ref, vc_ref, o_ref, *, n):
    blk = pl.program_id(1)
    rows = BAND_GROUPS * WINDOW
    start = pl.multiple_of(jnp.clip((blk - 1) * WINDOW, 0, n - BAND_SPAN), WINDOW)
    qpos = blk * WINDOW + lax.rem(lax.broadcasted_iota(jnp.int32, (rows, BAND_SPAN), 0), WINDOW)
    kpos = start + lax.broadcasted_iota(jnp.int32, (rows, BAND_SPAN), 1)
    in_band = jnp.abs(qpos - kpos) <= WINDOW
    for h in range(HKV_D):
        q = q_ref[h * BAND_GROUPS:(h + 1) * BAND_GROUPS].reshape(rows, HEAD_DIM)
        sink = _sink_column(sink_ref, h * BAND_GROUPS, BAND_GROUPS, WINDOW)
        o = _windowed_softmax(q, k_ref[h, pl.ds(start, BAND_SPAN), :], v_ref[h, pl.ds(start, BAND_SPAN), :],
                              kc_ref[h], vc_ref[h], mask=in_band, sink=sink)
        for g in range(BAND_GROUPS):
            col = (h * BAND_GROUPS + g) * HEAD_DIM
            o_ref[:, col:col + HEAD_DIM] = o[g * WINDOW:(g + 1) * WINDOW, :].astype(o_ref.dtype)


def _band_call(qkv, sink, ck, cv, li, nb, n):
    nblk = n // WINDOW
    m = ck.shape[3]
    return pl.pallas_call(
        functools.partial(_band_kernel, n=n),
        out_shape=jax.ShapeDtypeStruct((nb * n, H_D * HEAD_DIM), BF16),
        grid=(nb, nblk),
        in_specs=[
            pl.BlockSpec(memory_space=pltpu.SMEM),
            pl.BlockSpec((H_D, WINDOW, HEAD_DIM), lambda bb, i: (24 // H_D, bb * nblk + i, 0)),
            pl.BlockSpec((HKV_D, n, HEAD_DIM), lambda bb, i: (32 // HKV_D, bb, 0)),
            pl.BlockSpec((HKV_D, n, HEAD_DIM), lambda bb, i: (34 // HKV_D, bb, 0)),
            pl.BlockSpec((None, None, HKV_D, m, HEAD_DIM), lambda bb, i: (bb, li, H_C // HKV_D, 0, 0)),
            pl.BlockSpec((None, None, HKV_D, m, HEAD_DIM), lambda bb, i: (bb, li, H_C // HKV_D, 0, 0)),
        ],
        out_specs=pl.BlockSpec((WINDOW, H_D * HEAD_DIM), lambda bb, i: (bb * nblk + i, 0)),
        compiler_params=_params("parallel", "parallel"),
        name="band_attn",
    )(sink, qkv, qkv, qkv, ck, cv)


def _outproj_kernel(x_ref, mod_ref, ma_ref, mb_ref, w_ref, g_ref, b_ref, o_ref):
    gate = mod_ref[0, 5:6, :]
    for r0 in range(0, o_ref.shape[1], OUTPROJ_ROW_SLAB):
        rows = slice(r0, r0 + OUTPROJ_ROW_SLAB)
        f = _dot(ma_ref[rows, :], w_ref[:MIX_HALF, :]) + _dot(mb_ref[rows, :], w_ref[MIX_HALF:, :])
        o_ref[0, rows, :] = _layernorm(DN_ALPHA * x_ref[0, rows, :] + gate * f, g_ref[...], b_ref[...])


def _outproj_call(x, mods, mix_a, mix_b, w_out, ln_g, ln_b, li, tm=1024):
    nb, n, d = x.shape
    tm = min(tm, n)
    assert n % tm == 0 and tm % OUTPROJ_ROW_SLAB == 0
    tiles = n // tm
    return pl.pallas_call(
        _outproj_kernel,
        out_shape=jax.ShapeDtypeStruct(x.shape, F32),
        grid=(nb, tiles),
        in_specs=[
            pl.BlockSpec((1, tm, d), lambda bb, i: (bb, i, 0)),
            pl.BlockSpec((1, N_MOD, d), lambda bb, i: (bb, 0, 0)),
            pl.BlockSpec((tm, MIX_HALF), lambda bb, i: (bb * tiles + i, 0)),
            pl.BlockSpec((tm, MIX_HALF), lambda bb, i: (bb * tiles + i, 0)),
            pl.BlockSpec((None, 2 * MIX_HALF, d), lambda bb, i: (li, 0, 0), pipeline_mode=pl.Buffered(1)),
            pl.BlockSpec((1, d), lambda bb, i: (0, 0)),
            pl.BlockSpec((1, d), lambda bb, i: (0, 0)),
        ],
        out_specs=pl.BlockSpec((1, tm, d), lambda bb, i: (bb, i, 0)),
        compiler_params=_params("parallel", "parallel"),
        name="out_proj",
    )(x, mods, mix_a, mix_b, w_out, ln_g[li, 1][None, :], ln_b[li, 1][None, :])


def _rope_tables(n):
    t = jnp.arange(n, dtype=jnp.int32)
    rowcol = jnp.stack([t // GRID_W, t % GRID_W], axis=-1).astype(F32)
    lane = np.arange(HEAD_DIM)
    out = []
    for half in (32, 16):
        which = (lane // (2 * half)) % 2
        idx = lane % half
        low = (lane % (2 * half)) < half
        freqs = ROPE_THETA ** (-jnp.arange(half, dtype=F32) / half)
        ang = rowcol[:, which] * freqs[idx][None, :]
        cos, sin = jnp.cos(ang), jnp.sin(ang)
        out += [cos, jnp.where(low[None, :], -sin, 0.0), jnp.where(low[None, :], 0.0, sin)]
    return jnp.stack(out, axis=0)


def kernel(x_prompt, x_sample, cache_k, cache_v, c, c_ctx, w_ada, b_ada, ln_g, ln_b, ffn_w1, ffn_w3, ffn_w2,
           w_in, w_out, qk_gain_a, lam_b, subln_b, rpb_c, sink_d):
    batch, seq, d = x_prompt.shape
    dec_batch, dec_seq, _ = x_sample.shape

    w1, w3, w2 = ffn_w1.astype(BF16), ffn_w3.astype(BF16), ffn_w2.astype(BF16)
    w_in_b, w_out_b = w_in.astype(BF16), w_out.astype(BF16)
    ck_b, cv_b = cache_k.astype(BF16), cache_v.astype(BF16)

    cond_rows = 16
    cond = jnp.zeros((cond_rows, d), F32).at[:dec_batch].set(c).at[dec_batch].set(c_ctx)
    mods = _mods_call(cond, w_ada, b_ada).reshape(DEPTH, cond_rows, N_MOD, d)

    rope = _rope_tables(dec_seq)
    rows = dec_seq // GRID_W
    no_gain = jnp.ones((2, HEAD_DIM), F32)

    xp = x_prompt.reshape(1, batch * seq, d)
    xs = x_sample
    new_k, new_v = [], []
    for li in range(DEPTH):
        even = li % 2 == 0
        e = li // 2
        mods_s = mods[li, :dec_batch]
        mods_p = mods[li, dec_batch:dec_batch + 1]
        gains = qk_gain_a[e] if even else no_gain

        xp = _ffn_call(xp, mods_p, w1, w3, w2, ln_g, ln_b, li, 0)
        qkv, ck_new, cv_new = _proj_call(xp, mods_p, w_in_b, gains, None, li, seq=seq)
        new_k.append(ck_new)
        new_v.append(cv_new)
        if even:
            mix_a = _dense_call(qkv, 0, 8, 10, HKV_A, H_A // HKV_A, batch, seq, seq, seq)
            mix_b = _diff_call(qkv, lam_b, subln_b, li, batch, seq, seq, seq)
        else:
            mix_a = _dense_call(qkv, 0, 8, 16, H_C, 1, batch, seq, seq, seq)
            mix_b = _dense_call(qkv, 24, 32, 34, HKV_D, H_D // HKV_D, batch, seq, seq, seq, sink=sink_d[e])
        xp = _outproj_call(xp, mods_p, mix_a, mix_b, w_out_b, ln_g, ln_b, li)
        xp = _ffn_call(xp, mods_p, w1, w3, w2, ln_g, ln_b, li, 1)

        xs = _ffn_call(xs, mods_s, w1, w3, w2, ln_g, ln_b, li, 0)
        (qkv,) = _proj_call(xs, mods_s, w_in_b, gains, rope, li)
        if even:
            mix_a = _dense_call(qkv, 0, 8, 10, HKV_A, H_A // HKV_A, dec_batch, dec_seq, 256, 512,
                                ctx=(ck_b, cv_b, li, 0))
            mix_b = _diff_call(qkv, lam_b, subln_b, li, dec_batch, dec_seq, 1024, 512, ctx=(ck_b, cv_b))
        else:
            mix_a = _na_call(qkv, _na_bias(rpb_c[e], rows), ck_b, cv_b, li, dec_batch, dec_seq)
            mix_b = _band_call(qkv, sink_d[e], ck_b, cv_b, li, dec_batch, dec_seq)
        xs = _outproj_call(xs, mods_s, mix_a, mix_b, w_out_b, ln_g, ln_b, li)
        xs = _ffn_call(xs, mods_s, w1, w3, w2, ln_g, ln_b, li, 1)

    y_prompt = xp.reshape(batch, seq, d)
    new_cache_k = jnp.stack(new_k, axis=1)
    new_cache_v = jnp.stack(new_v, axis=1)
    return (y_prompt, xs, new_cache_k, new_cache_v)
```

```python
import functools
import math

import jax
import jax.numpy as jnp
import numpy as np
from jax import lax
from jax.experimental import pallas as pl
from jax.experimental.pallas import tpu as pltpu

D_MODEL = 2048
DEPTH = 4
GRID_W = 64
HEAD_DIM = 128
DIFF_HALF = HEAD_DIM // 2
H_A = 8
HKV_A = 2
H_B = 8
H_C = 8
H_D = 8
HKV_D = 2
KV_HEADS = HKV_A + H_B
D_FF = 5632
NA_ROWS = 8
NA_COLS = 16
WINDOW = 128
ROPE_THETA = 10000.0
LN_EPS = 1e-6
N_MOD = 9
DN_ALPHA = (2 * DEPTH) ** 0.25
PROJ_W = H_A * HEAD_DIM + 2 * HKV_A * HEAD_DIM + 3 * H_B * HEAD_DIM
N_PROJ_HEADS = PROJ_W // HEAD_DIM
MIX_HALF = H_A * HEAD_DIM

V7X_VMEM_BYTES = 64 * 1024 * 1024
VMEM_LIMIT_BYTES = V7X_VMEM_BYTES - 8 * 1024 * 1024
FFN_VMEM_LIMIT_BYTES = V7X_VMEM_BYTES - 3 * 1024 * 1024
LANES = 128
MASK_VALUE = -1e30
LOG2E = math.log2(math.e)
FFN_COL_CHUNK = 512
FFN_ROW_SLAB = 512
OUTPROJ_ROW_SLAB = 256

BF16 = jnp.bfloat16
F32 = jnp.float32


def _params(*semantics, vmem_limit=VMEM_LIMIT_BYTES):
    return pltpu.CompilerParams(dimension_semantics=semantics, vmem_limit_bytes=vmem_limit)


def _dot(a, b):
    return jnp.dot(a, b, preferred_element_type=F32)


def _dot_nt(a, b):
    return lax.dot_general(a, b, (((1,), (1,)), ((), ())), preferred_element_type=F32)


def _layernorm(y, g, b):
    mu = jnp.mean(y, axis=-1, keepdims=True)
    yc = y - mu
    var = jnp.mean(yc * yc, axis=-1, keepdims=True)
    return yc * lax.rsqrt(var + LN_EPS) * g + b


def _rmsnorm(t, g):
    return t * lax.rsqrt(jnp.mean(t * t, axis=-1, keepdims=True) + LN_EPS) * g


def _mods_kernel(c_ref, w_ref, b_ref, o_ref):
    c = c_ref[...]
    s = (c / (1.0 + jnp.exp(-c))).astype(BF16)
    o_ref[...] = _dot(s, w_ref[...].astype(BF16)) + b_ref[...]


def _mods_call(cond, w_ada, b_ada, tn=1024):
    rows = cond.shape[0]
    width = N_MOD * D_MODEL
    return pl.pallas_call(
        _mods_kernel,
        out_shape=jax.ShapeDtypeStruct((DEPTH, rows, width), F32),
        grid=(DEPTH, width // tn),
        in_specs=[
            pl.BlockSpec((rows, D_MODEL), lambda l, j: (0, 0)),
            pl.BlockSpec((None, D_MODEL, tn), lambda l, j: (l, 0, j)),
            pl.BlockSpec((None, 1, tn), lambda l, j: (l, 0, j)),
        ],
        out_specs=pl.BlockSpec((None, rows, tn), lambda l, j: (l, 0, j)),
        compiler_params=_params("parallel", "parallel"),
        name="adaln_mods",
    )(cond, w_ada, b_ada.reshape(DEPTH, 1, width))


def _ffn_kernel(x_ref, mod_ref, w1_ref, w3_ref, w2_ref, g_ref, b_ref, o_ref, h_ref, *, mod_base):
    j = pl.program_id(2)
    last = pl.num_programs(2) - 1

    def step(first, final):
        for r0 in range(0, o_ref.shape[1], FFN_ROW_SLAB):
            rows = slice(r0, r0 + FFN_ROW_SLAB)
            if first:
                shift = mod_ref[0, mod_base:mod_base + 1, :]
                scale = mod_ref[0, mod_base + 1:mod_base + 2, :]
                h = (x_ref[0, rows, :] * (1.0 + scale) + shift).astype(BF16)
                h_ref[rows, :] = h
            else:
                h = h_ref[rows, :]
            a = _dot(h, w1_ref[...])
            b = _dot(h, w3_ref[...])
            u = (a / (1.0 + jnp.exp(-a)) * b).astype(BF16)
            for c0 in range(0, o_ref.shape[2], FFN_COL_CHUNK):
                cols = slice(c0, c0 + FFN_COL_CHUNK)
                part = _dot(u, w2_ref[:, cols])
                if first:
                    o_ref[0, rows, cols] = part
                else:
                    o_ref[0, rows, cols] += part
            if final:
                gate = mod_ref[0, mod_base + 2:mod_base + 3, :]
                y = DN_ALPHA * x_ref[0, rows, :] + (0.5 * gate) * o_ref[0, rows, :]
                o_ref[0, rows, :] = _layernorm(y, g_ref[...], b_ref[...])

    pl.when(j == 0)(functools.partial(step, True, False))
    pl.when(jnp.logical_and(j > 0, j < last))(functools.partial(step, False, False))
    pl.when(j == last)(functools.partial(step, False, True))


def _ffn_call(x, mods, w1, w3, w2, ln_g, ln_b, li, slot, tm=1024, tf=512):
    nb, n, d = x.shape
    tm = min(tm, n)
    assert n % tm == 0 and tm % FFN_ROW_SLAB == 0 and w1.shape[-1] // tf >= 2
    mod_base = 0 if slot == 0 else 6
    ln_idx = 0 if slot == 0 else 2
    g = ln_g[li, ln_idx][None, :]
    b = ln_b[li, ln_idx][None, :]
    return pl.pallas_call(
        functools.partial(_ffn_kernel, mod_base=mod_base),
        out_shape=jax.ShapeDtypeStruct(x.shape, F32),
        grid=(nb, n // tm, w1.shape[-1] // tf),
        in_specs=[
            pl.BlockSpec((1, tm, d), lambda bb, i, j: (bb, i, 0)),
            pl.BlockSpec((1, N_MOD, d), lambda bb, i, j: (bb, 0, 0)),
            pl.BlockSpec((None, None, d, tf), lambda bb, i, j: (li, slot, 0, j)),
            pl.BlockSpec((None, None, d, tf), lambda bb, i, j: (li, slot, 0, j)),
            pl.BlockSpec((None, None, tf, d), lambda bb, i, j: (li, slot, j, 0)),
            pl.BlockSpec((1, d), lambda bb, i, j: (0, 0)),
            pl.BlockSpec((1, d), lambda bb, i, j: (0, 0)),
        ],
        out_specs=pl.BlockSpec((1, tm, d), lambda bb, i, j: (bb, i, 0)),
        scratch_shapes=[pltpu.VMEM((tm, d), BF16)],
        compiler_params=_params("parallel", "parallel", "arbitrary", vmem_limit=FFN_VMEM_LIMIT_BYTES),
        name="ffn",
    )(x, mods, w1, w3, w2, g, b)


Q_SCALE = HEAD_DIM ** -0.5 * LOG2E
Q_SCALE_DIFF = DIFF_HALF ** -0.5 * LOG2E
_EVEN_GROUPS = ((H_A, 0, "axial", Q_SCALE), (HKV_A, 1, "axial", None), (HKV_A, None, None, None),
                (H_B, None, "diff", Q_SCALE_DIFF), (H_B, None, "diff", None), (H_B, None, None, None))
_ODD_GROUPS = ((H_C, None, None, Q_SCALE), (H_C, None, None, None), (H_C, None, None, None),
               (H_D, None, "axial", Q_SCALE), (HKV_D, None, "axial", None), (HKV_D, None, None, None))
_EVEN_CACHE = (tuple(range(8, 10)) + tuple(range(20, 28)), tuple(range(10, 12)) + tuple(range(28, 36)))
_ODD_CACHE = (tuple(range(8, 16)) + tuple(range(32, 34)), tuple(range(16, 24)) + tuple(range(34, 36)))


def _rotate(t, rope_ref, kind):
    base, half = (0, 32) if kind == "axial" else (3, 16)
    up = pltpu.roll(t, LANES - half, 1)
    down = pltpu.roll(t, half, 1)
    return t * rope_ref[base] + up * rope_ref[base + 1] + down * rope_ref[base + 2]


def _proj_kernel(*refs, groups, use_rope, cache_heads, seq):
    x_ref, mod_ref, w_ref, gain_ref = refs[:4]
    pos = 4
    rope_ref = None
    if use_rope:
        rope_ref = refs[pos]
        pos += 1
    qkv_ref = refs[pos]
    pos += 1
    ck_ref = cv_ref = None
    if cache_heads is not None:
        ck_ref, cv_ref = refs[pos], refs[pos + 1]

    shift = mod_ref[0, 3:4, :]
    scale = mod_ref[0, 4:5, :]
    h = (x_ref[0] * (1.0 + scale) + shift).astype(BF16)
    tm = h.shape[0]
    head = 0
    for nheads, gain_row, kind, q_scale in groups:
        c0 = head * HEAD_DIM
        acc = _dot(h, w_ref[:, c0:c0 + nheads * HEAD_DIM])
        for k in range(nheads):
            t = acc[:, k * HEAD_DIM:(k + 1) * HEAD_DIM]
            if gain_row is not None:
                t = _rmsnorm(t, gain_ref[gain_row:gain_row + 1, :])
            if use_rope and kind is not None:
                t = _rotate(t, rope_ref, kind)
            qkv_ref[head + k] = (t if q_scale is None else t * q_scale).astype(BF16)
            if cache_heads is not None:
                for dst_ref, heads in ((ck_ref, cache_heads[0]), (cv_ref, cache_heads[1])):
                    if head + k in heads:
                        slot = heads.index(head + k)
                        for s in range(tm // seq):
                            dst_ref[s, slot] = t[s * seq:(s + 1) * seq, :]
        head += nheads


def _proj_call(x, mods, w_in, gains, rope, li, seq=None, tm=512):
    nb, n, d = x.shape
    assert n % tm == 0 and (seq is None or tm % seq == 0)
    tiles = n // tm
    even = li % 2 == 0
    groups = _EVEN_GROUPS if even else _ODD_GROUPS
    use_rope = rope is not None
    cache_heads = None if use_rope else (_EVEN_CACHE if even else _ODD_CACHE)
    in_specs = [
        pl.BlockSpec((1, tm, d), lambda bb, i: (bb, i, 0)),
        pl.BlockSpec((1, N_MOD, d), lambda bb, i: (bb, 0, 0)),
        pl.BlockSpec((None, d, PROJ_W), lambda bb, i: (li, 0, 0), pipeline_mode=pl.Buffered(1)),
        pl.BlockSpec((2, HEAD_DIM), lambda bb, i: (0, 0)),
    ]
    args = [x, mods, w_in, gains]
    if use_rope:
        in_specs.append(pl.BlockSpec((6, tm, HEAD_DIM), lambda bb, i: (0, i, 0)))
        args.append(rope)
    out_shape = [jax.ShapeDtypeStruct((N_PROJ_HEADS, nb * n, HEAD_DIM), BF16)]
    out_specs = [pl.BlockSpec((N_PROJ_HEADS, tm, HEAD_DIM), lambda bb, i: (0, bb * tiles + i, 0))]
    if cache_heads is not None:
        per = tm // seq
        for _ in range(2):
            out_shape.append(jax.ShapeDtypeStruct((nb * n // seq, KV_HEADS, seq, HEAD_DIM), F32))
            out_specs.append(pl.BlockSpec((per, KV_HEADS, seq, HEAD_DIM), lambda bb, i: (bb * tiles + i, 0, 0, 0)))
    return pl.pallas_call(
        functools.partial(_proj_kernel, groups=groups, use_rope=use_rope, cache_heads=cache_heads, seq=seq),
        out_shape=out_shape,
        grid=(nb, tiles),
        in_specs=in_specs,
        out_specs=out_specs,
        compiler_params=_params("parallel", "parallel"),
        name="qkv_proj",
    )(*args)


def _softmax_step(state, s, v):
    row_max = jnp.max(s, axis=-1, keepdims=True)
    v1 = _with_ones(v)
    if state is None:
        p = jnp.exp2(s - row_max)
        return row_max, _dot(p.astype(BF16), v1)
    m, acc = state
    m_new = jnp.maximum(m, row_max)
    alpha = jnp.exp2(m - m_new)
    p = jnp.exp2(s - m_new)
    return m_new, alpha * acc + _dot(p.astype(BF16), v1)


def _with_ones(v):
    return jnp.concatenate([v, jnp.ones(v.shape, v.dtype)], axis=1)


def _normalize(acc, extra=None):
    den = acc[:, HEAD_DIM:]
    if extra is not None:
        den = den + extra
    return acc[:, :HEAD_DIM] / den


def _sink_column(sink_ref, first, groups, rows_per_head):
    row = lax.broadcasted_iota(jnp.int32, (groups * rows_per_head, 1), 0)
    col = jnp.full((groups * rows_per_head, 1), sink_ref[first], F32)
    for g in range(1, groups):
        col = jnp.where(row >= g * rows_per_head, sink_ref[first + g], col)
    return col * LOG2E


def _dense_kernel(*refs, hps, groups, tq, nk, tk, has_ctx, has_sink):
    pos = 0
    sink_ref = None
    if has_sink:
        sink_ref = refs[0]
        pos = 1
    q_ref, k_ref, v_ref = refs[pos:pos + 3]
    pos += 3
    if has_ctx:
        kc_ref, vc_ref = refs[pos:pos + 2]
        pos += 2
    o_ref = refs[pos]

    for hh in range(hps):
        q = q_ref[hh * groups:(hh + 1) * groups].reshape(groups * tq, HEAD_DIM)
        state = None
        for c in range(nk // tk):
            s = _dot_nt(q, k_ref[hh, c * tk:(c + 1) * tk, :])
            state = _softmax_step(state, s, v_ref[hh, c * tk:(c + 1) * tk, :])
        if has_ctx:
            s = _dot_nt(q, kc_ref[hh])
            state = _softmax_step(state, s, vc_ref[hh])
        m, acc = state
        extra = None
        if has_sink:
            sink = _sink_column(sink_ref, (pl.program_id(1) * hps + hh) * groups, groups, tq)
            m_new = jnp.maximum(m, sink)
            acc = jnp.exp2(m - m_new) * acc
            extra = jnp.exp2(sink - m_new)
        o = _normalize(acc, extra)
        for g in range(groups):
            col = (hh * groups + g) * HEAD_DIM
            o_ref[:, col:col + HEAD_DIM] = o[g * tq:(g + 1) * tq, :].astype(o_ref.dtype)


def _dense_call(qkv, q_head0, k_head0, v_head0, n_kv, groups, nb, n, tq, tk, hps=1, ctx=None, sink=None):
    qt = n // tq
    assert n_kv % hps == 0 and q_head0 % (hps * groups) == 0 and k_head0 % hps == 0 and v_head0 % hps == 0
    in_specs, args = [], []
    if sink is not None:
        in_specs.append(pl.BlockSpec(memory_space=pltpu.SMEM))
        args.append(sink)
    qh0, kh0, vh0 = q_head0 // (hps * groups), k_head0 // hps, v_head0 // hps
    in_specs += [
        pl.BlockSpec((hps * groups, tq, HEAD_DIM), lambda bb, h, i: (qh0 + h, bb * qt + i, 0)),
        pl.BlockSpec((hps, n, HEAD_DIM), lambda bb, h, i: (kh0 + h, bb, 0)),
        pl.BlockSpec((hps, n, HEAD_DIM), lambda bb, h, i: (vh0 + h, bb, 0)),
    ]
    args += [qkv, qkv, qkv]
    if ctx is not None:
        ck, cv, li, ch0 = ctx
        assert ch0 % hps == 0
        m = ck.shape[3]
        for arr in (ck, cv):
            in_specs.append(pl.BlockSpec((None, None, hps, m, HEAD_DIM),
                                         lambda bb, h, i: (bb, li, ch0 // hps + h, 0, 0)))
            args.append(arr)
    kern = functools.partial(_dense_kernel, hps=hps, groups=groups, tq=tq, nk=n, tk=tk, has_ctx=ctx is not None,
                             has_sink=sink is not None)
    return pl.pallas_call(
        kern,
        out_shape=jax.ShapeDtypeStruct((nb * n, n_kv * groups * HEAD_DIM), BF16),
        grid=(nb, n_kv // hps, qt),
        in_specs=in_specs,
        out_specs=pl.BlockSpec((tq, hps * groups * HEAD_DIM), lambda bb, h, i: (bb * qt + i, h)),
        compiler_params=_params("parallel", "parallel", "parallel"),
        name="dense_attn",
    )(*args)


def _diff_kernel(*refs, hps, nk, tk, has_ctx, lam_init):
    q_ref, k_ref, v_ref = refs[:3]
    pos = 3
    if has_ctx:
        kc_ref, vc_ref = refs[pos:pos + 2]
        pos += 2
    lam_ref, gain_ref, o_ref = refs[pos:pos + 3]

    lb = lam_ref[...]
    lam = (jnp.exp(jnp.sum(lb[0:1] * lb[1:2], axis=-1, keepdims=True))
           - jnp.exp(jnp.sum(lb[2:3] * lb[3:4], axis=-1, keepdims=True)) + lam_init)
    for hh in range(hps):
        q = q_ref[hh]
        lane = lax.broadcasted_iota(jnp.int32, q.shape, 1)
        zero = jnp.zeros_like(q)
        q1 = jnp.where(lane < DIFF_HALF, q, zero)
        q2 = jnp.where(lane >= DIFF_HALF, q, zero)
        st1 = st2 = None
        for c in range(nk // tk):
            kc = k_ref[hh, c * tk:(c + 1) * tk, :]
            vc = v_ref[hh, c * tk:(c + 1) * tk, :]
            st1 = _softmax_step(st1, _dot_nt(q1, kc), vc)
            st2 = _softmax_step(st2, _dot_nt(q2, kc), vc)
        if has_ctx:
            st1 = _softmax_step(st1, _dot_nt(q1, kc_ref[hh]), vc_ref[hh])
            st2 = _softmax_step(st2, _dot_nt(q2, kc_ref[hh]), vc_ref[hh])
        o = _normalize(st1[1]) - lam * _normalize(st2[1])
        o = _rmsnorm(o, gain_ref[...]) * (1.0 - lam_init)
        o_ref[:, hh * HEAD_DIM:(hh + 1) * HEAD_DIM] = o.astype(o_ref.dtype)


def _diff_call(qkv, lam_b, subln, li, nb, n, tq, tk, hps=1, ctx=None):
    qt = n // tq
    e = li // 2
    lam_init = 0.8 - 0.6 * math.exp(-0.3 * li)
    assert H_B % hps == 0 and 12 % hps == 0 and (ctx is None or HKV_A % hps == 0)
    in_specs = [
        pl.BlockSpec((hps, tq, HEAD_DIM), lambda bb, h, i: (12 // hps + h, bb * qt + i, 0)),
        pl.BlockSpec((hps, n, HEAD_DIM), lambda bb, h, i: (20 // hps + h, bb, 0)),
        pl.BlockSpec((hps, n, HEAD_DIM), lambda bb, h, i: (28 // hps + h, bb, 0)),
    ]
    args = [qkv, qkv, qkv]
    if ctx is not None:
        ck, cv = ctx
        m = ck.shape[3]
        for arr in (ck, cv):
            in_specs.append(pl.BlockSpec((None, None, hps, m, HEAD_DIM),
                                         lambda bb, h, i: (bb, li, HKV_A // hps + h, 0, 0)))
            args.append(arr)
    in_specs += [
        pl.BlockSpec((None, 4, DIFF_HALF), lambda bb, h, i: (e, 0, 0)),
        pl.BlockSpec((None, 1, HEAD_DIM), lambda bb, h, i: (e, 0, 0)),
    ]
    args += [lam_b, subln.reshape(-1, 1, HEAD_DIM)]
    return pl.pallas_call(
        functools.partial(_diff_kernel, hps=hps, nk=n, tk=tk, has_ctx=ctx is not None, lam_init=lam_init),
        out_shape=jax.ShapeDtypeStruct((nb * n, H_B * HEAD_DIM), BF16),
        grid=(nb, H_B // hps, qt),
        in_specs=in_specs,
        out_specs=pl.BlockSpec((tq, hps * HEAD_DIM), lambda bb, h, i: (bb * qt + i, h)),
        compiler_params=_params("parallel", "parallel", "parallel"),
        name="diff_attn",
    )(*args)


NA_Q_ROWS = 4
NA_K_ROWS = NA_Q_ROWS + NA_ROWS


def _na_window_start(j, rows):
    return jnp.clip(NA_Q_ROWS * j - NA_ROWS // 2, 0, rows - NA_K_ROWS)


def _na_bias(rpb, rows):
    nj = rows // NA_Q_ROWS
    heads = rpb.shape[0]
    n_ro, n_co = 2 * NA_ROWS - 1, 2 * NA_COLS - 1
    c = np.arange(GRID_W)[:, None]
    kc = np.arange(GRID_W)[None, :]
    cs = np.clip(c - NA_COLS // 2, 0, GRID_W - NA_COLS)
    col_valid = (kc >= cs) & (kc < cs + NA_COLS)
    onehot = (kc - c + NA_COLS - 1)[None] == np.arange(n_co)[:, None, None]
    rpb2 = rpb.astype(F32) * LOG2E
    band = jnp.sum(jnp.where(onehot[None, None], rpb2[:, :, :, None, None], 0.0), axis=2)
    band = jnp.where(col_valid[None, None], band, MASK_VALUE)
    blocks = jnp.concatenate([band, jnp.full((heads, 1, GRID_W, GRID_W), MASK_VALUE, F32)], axis=1)
    pick = np.full((3, NA_Q_ROWS, NA_K_ROWS), n_ro, np.int32)
    for v, j in enumerate((0, 1, nj - 1)):
        ws = int(np.clip(NA_Q_ROWS * j - NA_ROWS // 2, 0, rows - NA_K_ROWS))
        for rq in range(NA_Q_ROWS):
            r = NA_Q_ROWS * j + rq
            rs = int(np.clip(r - NA_ROWS // 2, 0, rows - NA_ROWS))
            for kr in range(NA_K_ROWS):
                if rs <= ws + kr < rs + NA_ROWS:
                    pick[v, rq, kr] = ws + kr - r + NA_ROWS - 1
    rows_of_blocks = []
    for v in range(3):
        for rq in range(NA_Q_ROWS):
            rows_of_blocks.append(jnp.concatenate([blocks[:, int(pick[v, rq, kr])] for kr in range(NA_K_ROWS)], axis=-1))
    table = jnp.stack(rows_of_blocks, axis=1)
    return table.reshape(heads, 3, NA_Q_ROWS * GRID_W, NA_K_ROWS * GRID_W)


def _windowed_softmax(q, k_loc, v_loc, k_ctx, v_ctx, bias=None, mask=None, sink=None):
    s_loc = _dot_nt(q, k_loc)
    if bias is not None:
        s_loc = s_loc + bias
    if mask is not None:
        s_loc = jnp.where(mask, s_loc, MASK_VALUE)
    s_ctx = _dot_nt(q, k_ctx)
    m = jnp.maximum(jnp.max(s_loc, axis=-1, keepdims=True), jnp.max(s_ctx, axis=-1, keepdims=True))
    if sink is not None:
        m = jnp.maximum(m, sink)
    p_loc = jnp.exp2(s_loc - m)
    p_ctx = jnp.exp2(s_ctx - m)
    l = jnp.sum(p_loc, axis=-1, keepdims=True) + jnp.sum(p_ctx, axis=-1, keepdims=True)
    if sink is not None:
        l = l + jnp.exp2(sink - m)
    return (_dot(p_loc.astype(BF16), v_loc) + _dot(p_ctx.astype(BF16), v_ctx)) / l


def _na_kernel(q_ref, k_ref, v_ref, kc_ref, vc_ref, bias_ref, o_ref):
    for h in range(H_C):
        o = _windowed_softmax(q_ref[h], k_ref[h], v_ref[h], kc_ref[h], vc_ref[h], bias=bias_ref[h])
        o_ref[:, h * HEAD_DIM:(h + 1) * HEAD_DIM] = o.astype(o_ref.dtype)


def _na_call(qkv, bias, ck, cv, li, nb, n):
    rows = n // GRID_W
    nj = rows // NA_Q_ROWS
    tq = NA_Q_ROWS * GRID_W
    span = NA_K_ROWS * GRID_W
    m = ck.shape[3]

    window = (pl.Element(H_C), pl.Element(span), pl.Element(HEAD_DIM))

    def window_map(first_head):
        return lambda bb, j: (first_head, pl.multiple_of(bb * n + _na_window_start(j, rows) * GRID_W, GRID_W), 0)

    def bias_map(bb, j):
        return (0, jnp.where(j == 0, 0, jnp.where(j == nj - 1, 2, 1)), 0, 0)

    return pl.pallas_call(
        _na_kernel,
        out_shape=jax.ShapeDtypeStruct((nb * n, H_C * HEAD_DIM), BF16),
        grid=(nb, nj),
        in_specs=[
            pl.BlockSpec((H_C, tq, HEAD_DIM), lambda bb, j: (0, bb * nj + j, 0)),
            pl.BlockSpec(window, window_map(H_C)),
            pl.BlockSpec(window, window_map(2 * H_C)),
            pl.BlockSpec((None, None, H_C, m, HEAD_DIM), lambda bb, j: (bb, li, 0, 0, 0)),
            pl.BlockSpec((None, None, H_C, m, HEAD_DIM), lambda bb, j: (bb, li, 0, 0, 0)),
            pl.BlockSpec((H_C, None, tq, span), bias_map),
        ],
        out_specs=pl.BlockSpec((tq, H_C * HEAD_DIM), lambda bb, j: (bb * nj + j, 0)),
        compiler_params=_params("parallel", "parallel"),
        name="na2d_attn",
    )(qkv, qkv, qkv, ck, cv, bias)


BAND_SPAN = 3 * WINDOW
BAND_GROUPS = H_D // HKV_D


BAND_BLOCKS_PER_STEP = 4


def _band_kernel(sink_ref, q_ref, k_ref, v_ref, kc_ref, vc_ref, o_ref, *, n):
    rows = BAND_GROUPS * WINDOW
    for s in range(BAND_BLOCKS_PER_STEP):
        blk = pl.program_id(1) * BAND_BLOCKS_PER_STEP + s
        start = pl.multiple_of(jnp.clip((blk - 1) * WINDOW, 0, n - BAND_SPAN), WINDOW)
        qpos = blk * WINDOW + lax.rem(lax.broadcasted_iota(jnp.int32, (rows, BAND_SPAN), 0), WINDOW)
        kpos = start + lax.broadcasted_iota(jnp.int32, (rows, BAND_SPAN), 1)
        in_band = jnp.abs(qpos - kpos) <= WINDOW
        tok = slice(s * WINDOW, (s + 1) * WINDOW)
        for h in range(HKV_D):
            q = q_ref[h * BAND_GROUPS:(h + 1) * BAND_GROUPS, tok, :].reshape(rows, HEAD_DIM)
            sink = _sink_column(sink_ref, h * BAND_GROUPS, BAND_GROUPS, WINDOW)
            o = _windowed_softmax(q, k_ref[h, pl.ds(start, BAND_SPAN), :], v_ref[h, pl.ds(start, BAND_SPAN), :],
                                  kc_ref[h], vc_ref[h], mask=in_band, sink=sink)
            for g in range(BAND_GROUPS):
                col = (h * BAND_GROUPS + g) * HEAD_DIM
                o_ref[tok, col:col + HEAD_DIM] = o[g * WINDOW:(g + 1) * WINDOW, :].astype(o_ref.dtype)


def _band_call(qkv, sink, ck, cv, li, nb, n):
    tq = BAND_BLOCKS_PER_STEP * WINDOW
    nblk = n // tq
    m = ck.shape[3]
    return pl.pallas_call(
        functools.partial(_band_kernel, n=n),
        out_shape=jax.ShapeDtypeStruct((nb * n, H_D * HEAD_DIM), BF16),
        grid=(nb, nblk),
        in_specs=[
            pl.BlockSpec(memory_space=pltpu.SMEM),
            pl.BlockSpec((H_D, tq, HEAD_DIM), lambda bb, i: (24 // H_D, bb * nblk + i, 0)),
            pl.BlockSpec((HKV_D, n, HEAD_DIM), lambda bb, i: (32 // HKV_D, bb, 0)),
            pl.BlockSpec((HKV_D, n, HEAD_DIM), lambda bb, i: (34 // HKV_D, bb, 0)),
            pl.BlockSpec((None, None, HKV_D, m, HEAD_DIM), lambda bb, i: (bb, li, H_C // HKV_D, 0, 0)),
            pl.BlockSpec((None, None, HKV_D, m, HEAD_DIM), lambda bb, i: (bb, li, H_C // HKV_D, 0, 0)),
        ],
        out_specs=pl.BlockSpec((tq, H_D * HEAD_DIM), lambda bb, i: (bb * nblk + i, 0)),
        compiler_params=_params("parallel", "parallel"),
        name="band_attn",
    )(sink, qkv, qkv, qkv, ck, cv)


def _outproj_kernel(x_ref, mod_ref, ma_ref, mb_ref, w_ref, g_ref, b_ref, o_ref):
    gate = mod_ref[0, 5:6, :]
    for r0 in range(0, o_ref.shape[1], OUTPROJ_ROW_SLAB):
        rows = slice(r0, r0 + OUTPROJ_ROW_SLAB)
        f = _dot(ma_ref[rows, :], w_ref[:MIX_HALF, :]) + _dot(mb_ref[rows, :], w_ref[MIX_HALF:, :])
        o_ref[0, rows, :] = _layernorm(DN_ALPHA * x_ref[0, rows, :] + gate * f, g_ref[...], b_ref[...])


def _outproj_call(x, mods, mix_a, mix_b, w_out, ln_g, ln_b, li, tm=1024):
    nb, n, d = x.shape
    tm = min(tm, n)
    assert n % tm == 0 and tm % OUTPROJ_ROW_SLAB == 0
    tiles = n // tm
    return pl.pallas_call(
        _outproj_kernel,
        out_shape=jax.ShapeDtypeStruct(x.shape, F32),
        grid=(nb, tiles),
        in_specs=[
            pl.BlockSpec((1, tm, d), lambda bb, i: (bb, i, 0)),
            pl.BlockSpec((1, N_MOD, d), lambda bb, i: (bb, 0, 0)),
            pl.BlockSpec((tm, MIX_HALF), lambda bb, i: (bb * tiles + i, 0)),
            pl.BlockSpec((tm, MIX_HALF), lambda bb, i: (bb * tiles + i, 0)),
            pl.BlockSpec((None, 2 * MIX_HALF, d), lambda bb, i: (li, 0, 0), pipeline_mode=pl.Buffered(1)),
            pl.BlockSpec((1, d), lambda bb, i: (0, 0)),
            pl.BlockSpec((1, d), lambda bb, i: (0, 0)),
        ],
        out_specs=pl.BlockSpec((1, tm, d), lambda bb, i: (bb, i, 0)),
        compiler_params=_params("parallel", "parallel"),
        name="out_proj",
    )(x, mods, mix_a, mix_b, w_out, ln_g[li, 1][None, :], ln_b[li, 1][None, :])


def _rope_tables(n):
    t = jnp.arange(n, dtype=jnp.int32)
    rowcol = jnp.stack([t // GRID_W, t % GRID_W], axis=-1).astype(F32)
    lane = np.arange(HEAD_DIM)
    out = []
    for half in (32, 16):
        which = (lane // (2 * half)) % 2
        idx = lane % half
        low = (lane % (2 * half)) < half
        freqs = ROPE_THETA ** (-jnp.arange(half, dtype=F32) / half)
        ang = rowcol[:, which] * freqs[idx][None, :]
        cos, sin = jnp.cos(ang), jnp.sin(ang)
        out += [cos, jnp.where(low[None, :], -sin, 0.0), jnp.where(low[None, :], 0.0, sin)]
    return jnp.stack(out, axis=0)


def kernel(x_prompt, x_sample, cache_k, cache_v, c, c_ctx, w_ada, b_ada, ln_g, ln_b, ffn_w1, ffn_w3, ffn_w2,
           w_in, w_out, qk_gain_a, lam_b, subln_b, rpb_c, sink_d):
    batch, seq, d = x_prompt.shape
    dec_batch, dec_seq, _ = x_sample.shape

    w1, w3, w2 = ffn_w1.astype(BF16), ffn_w3.astype(BF16), ffn_w2.astype(BF16)
    w_in_b, w_out_b = w_in.astype(BF16), w_out.astype(BF16)
    ck_b, cv_b = cache_k.astype(BF16), cache_v.astype(BF16)

    cond_rows = 16
    cond = jnp.zeros((cond_rows, d), F32).at[:dec_batch].set(c).at[dec_batch].set(c_ctx)
    mods = _mods_call(cond, w_ada, b_ada).reshape(DEPTH, cond_rows, N_MOD, d)

    rope = _rope_tables(dec_seq)
    rows = dec_seq // GRID_W
    no_gain = jnp.ones((2, HEAD_DIM), F32)

    xp = x_prompt.reshape(1, batch * seq, d)
    xs = x_sample
    new_k, new_v = [], []
    for li in range(DEPTH):
        even = li % 2 == 0
        e = li // 2
        mods_s = mods[li, :dec_batch]
        mods_p = mods[li, dec_batch:dec_batch + 1]
        gains = qk_gain_a[e] if even else no_gain

        xp = _ffn_call(xp, mods_p, w1, w3, w2, ln_g, ln_b, li, 0)
        qkv, ck_new, cv_new = _proj_call(xp, mods_p, w_in_b, gains, None, li, seq=seq)
        new_k.append(ck_new)
        new_v.append(cv_new)
        if even:
            mix_a = _dense_call(qkv, 0, 8, 10, HKV_A, H_A // HKV_A, batch, seq, seq, seq, hps=HKV_A)
            mix_b = _diff_call(qkv, lam_b, subln_b, li, batch, seq, seq, seq, hps=4)
        else:
            mix_a = _dense_call(qkv, 0, 8, 16, H_C, 1, batch, seq, seq, seq, hps=H_C)
            mix_b = _dense_call(qkv, 24, 32, 34, HKV_D, H_D // HKV_D, batch, seq, seq, seq, hps=HKV_D,
                                sink=sink_d[e])
        xp = _outproj_call(xp, mods_p, mix_a, mix_b, w_out_b, ln_g, ln_b, li)
        xp = _ffn_call(xp, mods_p, w1, w3, w2, ln_g, ln_b, li, 1)

        xs = _ffn_call(xs, mods_s, w1, w3, w2, ln_g, ln_b, li, 0)
        (qkv,) = _proj_call(xs, mods_s, w_in_b, gains, rope, li)
        if even:
            mix_a = _dense_call(qkv, 0, 8, 10, HKV_A, H_A // HKV_A, dec_batch, dec_seq, 256, 1024,
                                ctx=(ck_b, cv_b, li, 0))
            mix_b = _diff_call(qkv, lam_b, subln_b, li, dec_batch, dec_seq, 1024, 512, ctx=(ck_b, cv_b))
        else:
            mix_a = _na_call(qkv, _na_bias(rpb_c[e], rows), ck_b, cv_b, li, dec_batch, dec_seq)
            mix_b = _band_call(qkv, sink_d[e], ck_b, cv_b, li, dec_batch, dec_seq)
        xs = _outproj_call(xs, mods_s, mix_a, mix_b, w_out_b, ln_g, ln_b, li)
        xs = _ffn_call(xs, mods_s, w1, w3, w2, ln_g, ln_b, li, 1)

    y_prompt = xp.reshape(batch, seq, d)
    new_cache_k = jnp.stack(new_k, axis=1)
    new_cache_v = jnp.stack(new_v, axis=1)
    return (y_prompt, xs, new_cache_k, new_cache_v)
```

```python
import functools
import math

import jax
import jax.numpy as jnp
import numpy as np
from jax import lax
from jax.experimental import pallas as pl
from jax.experimental.pallas import tpu as pltpu

D_MODEL = 2048
DEPTH = 4
GRID_W = 64
HEAD_DIM = 128
DIFF_HALF = HEAD_DIM // 2
H_A = 8
HKV_A = 2
H_B = 8
H_C = 8
H_D = 8
HKV_D = 2
KV_HEADS = HKV_A + H_B
NA_ROWS = 8
NA_COLS = 16
WINDOW = 128
ROPE_THETA = 10000.0
LN_EPS = 1e-6
N_MOD = 9
DN_ALPHA = (2 * DEPTH) ** 0.25
PROJ_W = H_A * HEAD_DIM + 2 * HKV_A * HEAD_DIM + 3 * H_B * HEAD_DIM
N_PROJ_HEADS = PROJ_W // HEAD_DIM
MIX_HALF = H_A * HEAD_DIM

V7X_VMEM_BYTES = 64 * 1024 * 1024
VMEM_LIMIT_BYTES = V7X_VMEM_BYTES - 8 * 1024 * 1024
FFN_VMEM_LIMIT_BYTES = V7X_VMEM_BYTES - 3 * 1024 * 1024
LANES = 128
MASK_VALUE = -1e30
LOG2E = math.log2(math.e)
FFN_COL_CHUNK = 512
FFN_ROW_SLAB = 512
OUTPROJ_ROW_SLAB = 256
COND_ROWS = 16
LATENT_DENSE_TILES = (256, 1024)
LATENT_DIFF_TILES = (1024, 512)

BF16 = jnp.bfloat16
F32 = jnp.float32


def _params(*semantics, vmem_limit=VMEM_LIMIT_BYTES):
    return pltpu.CompilerParams(dimension_semantics=semantics, vmem_limit_bytes=vmem_limit)


def _dot(a, b):
    return jnp.dot(a, b, preferred_element_type=F32)


def _dot_nt(a, b):
    return lax.dot_general(a, b, (((1,), (1,)), ((), ())), preferred_element_type=F32)


def _layernorm(y, g, b):
    mu = jnp.mean(y, axis=-1, keepdims=True)
    yc = y - mu
    var = jnp.mean(yc * yc, axis=-1, keepdims=True)
    return yc * lax.rsqrt(var + LN_EPS) * g + b


def _rmsnorm(t, g):
    return t * lax.rsqrt(jnp.mean(t * t, axis=-1, keepdims=True) + LN_EPS) * g


def _mods_kernel(c_ref, w_ref, b_ref, o_ref):
    c = c_ref[...]
    s = (c / (1.0 + jnp.exp(-c))).astype(BF16)
    o_ref[...] = _dot(s, w_ref[...].astype(BF16)) + b_ref[...]


def _mods_call(cond, w_ada, b_ada, tn=1024):
    rows = cond.shape[0]
    width = N_MOD * D_MODEL
    return pl.pallas_call(
        _mods_kernel,
        out_shape=jax.ShapeDtypeStruct((DEPTH, rows, width), F32),
        grid=(DEPTH, width // tn),
        in_specs=[
            pl.BlockSpec((rows, D_MODEL), lambda l, j: (0, 0)),
            pl.BlockSpec((None, D_MODEL, tn), lambda l, j: (l, 0, j)),
            pl.BlockSpec((None, 1, tn), lambda l, j: (l, 0, j)),
        ],
        out_specs=pl.BlockSpec((None, rows, tn), lambda l, j: (l, 0, j)),
        compiler_params=_params("parallel", "parallel"),
        name="adaln_mods",
    )(cond, w_ada, b_ada.reshape(DEPTH, 1, width))


def _ffn_kernel(x_ref, mod_ref, w1_ref, w3_ref, w2_ref, g_ref, b_ref, o_ref, h_ref, *, mod_base):
    j = pl.program_id(2)
    last = pl.num_programs(2) - 1

    def step(first, final):
        for r0 in range(0, o_ref.shape[1], FFN_ROW_SLAB):
            rows = slice(r0, r0 + FFN_ROW_SLAB)
            if first:
                shift = mod_ref[0, mod_base:mod_base + 1, :]
                scale = mod_ref[0, mod_base + 1:mod_base + 2, :]
                h = (x_ref[0, rows, :] * (1.0 + scale) + shift).astype(BF16)
                h_ref[rows, :] = h
            else:
                h = h_ref[rows, :]
            a = _dot(h, w1_ref[...])
            b = _dot(h, w3_ref[...])
            u = (a / (1.0 + jnp.exp(-a)) * b).astype(BF16)
            for c0 in range(0, o_ref.shape[2], FFN_COL_CHUNK):
                cols = slice(c0, c0 + FFN_COL_CHUNK)
                part = _dot(u, w2_ref[:, cols])
                if first:
                    o_ref[0, rows, cols] = part
                else:
                    o_ref[0, rows, cols] += part
            if final:
                gate = mod_ref[0, mod_base + 2:mod_base + 3, :]
                y = DN_ALPHA * x_ref[0, rows, :] + (0.5 * gate) * o_ref[0, rows, :]
                o_ref[0, rows, :] = _layernorm(y, g_ref[...], b_ref[...])

    pl.when(j == 0)(functools.partial(step, True, False))
    pl.when(jnp.logical_and(j > 0, j < last))(functools.partial(step, False, False))
    pl.when(j == last)(functools.partial(step, False, True))


def _ffn_call(x, mods, w1, w3, w2, ln_g, ln_b, li, slot, tm=1024, tf=512):
    nb, n, d = x.shape
    tm = min(tm, n)
    assert n % tm == 0 and tm % FFN_ROW_SLAB == 0 and w1.shape[-1] // tf >= 2
    mod_base = 0 if slot == 0 else 6
    ln_idx = 0 if slot == 0 else 2
    g = ln_g[li, ln_idx][None, :]
    b = ln_b[li, ln_idx][None, :]
    return pl.pallas_call(
        functools.partial(_ffn_kernel, mod_base=mod_base),
        out_shape=jax.ShapeDtypeStruct(x.shape, F32),
        grid=(nb, n // tm, w1.shape[-1] // tf),
        in_specs=[
            pl.BlockSpec((1, tm, d), lambda bb, i, j: (bb, i, 0)),
            pl.BlockSpec((1, N_MOD, d), lambda bb, i, j: (bb, 0, 0)),
            pl.BlockSpec((None, None, d, tf), lambda bb, i, j: (li, slot, 0, j)),
            pl.BlockSpec((None, None, d, tf), lambda bb, i, j: (li, slot, 0, j)),
            pl.BlockSpec((None, None, tf, d), lambda bb, i, j: (li, slot, j, 0)),
            pl.BlockSpec((1, d), lambda bb, i, j: (0, 0)),
            pl.BlockSpec((1, d), lambda bb, i, j: (0, 0)),
        ],
        out_specs=pl.BlockSpec((1, tm, d), lambda bb, i, j: (bb, i, 0)),
        scratch_shapes=[pltpu.VMEM((tm, d), BF16)],
        compiler_params=_params("parallel", "parallel", "arbitrary", vmem_limit=FFN_VMEM_LIMIT_BYTES),
        name="ffn",
    )(x, mods, w1, w3, w2, g, b)


Q_SCALE = HEAD_DIM ** -0.5 * LOG2E
Q_SCALE_DIFF = DIFF_HALF ** -0.5 * LOG2E
_EVEN_GROUPS = ((H_A, 0, "axial", Q_SCALE), (HKV_A, 1, "axial", None), (HKV_A, None, None, None),
                (H_B, None, "diff", Q_SCALE_DIFF), (H_B, None, "diff", None), (H_B, None, None, None))
_ODD_GROUPS = ((H_C, None, None, Q_SCALE), (H_C, None, None, None), (H_C, None, None, None),
               (H_D, None, "axial", Q_SCALE), (HKV_D, None, "axial", None), (HKV_D, None, None, None))


def _first_heads(groups):
    firsts, head = [], 0
    for group in groups:
        firsts.append(head)
        head += group[0]
    return tuple(firsts)


QA, KA, VA, QB, KB, VB = _first_heads(_EVEN_GROUPS)
QC, KC, VC, QD, KD, VD = _first_heads(_ODD_GROUPS)
_EVEN_CACHE = (tuple(range(KA, KA + HKV_A)) + tuple(range(KB, KB + H_B)),
               tuple(range(VA, VA + HKV_A)) + tuple(range(VB, VB + H_B)))
_ODD_CACHE = (tuple(range(KC, KC + H_C)) + tuple(range(KD, KD + HKV_D)),
              tuple(range(VC, VC + H_C)) + tuple(range(VD, VD + HKV_D)))


def _rotate(t, rope_ref, kind):
    base, half = (0, 32) if kind == "axial" else (3, 16)
    up = pltpu.roll(t, LANES - half, 1)
    down = pltpu.roll(t, half, 1)
    return t * rope_ref[base] + up * rope_ref[base + 1] + down * rope_ref[base + 2]


def _proj_kernel(*refs, groups, use_rope, cache_heads, seq):
    x_ref, mod_ref, w_ref, gain_ref = refs[:4]
    pos = 4
    rope_ref = None
    if use_rope:
        rope_ref = refs[pos]
        pos += 1
    qkv_ref = refs[pos]
    pos += 1
    ck_ref = cv_ref = None
    if cache_heads is not None:
        ck_ref, cv_ref = refs[pos], refs[pos + 1]

    shift = mod_ref[0, 3:4, :]
    scale = mod_ref[0, 4:5, :]
    h = (x_ref[0] * (1.0 + scale) + shift).astype(BF16)
    tm = h.shape[0]
    head = 0
    for nheads, gain_row, kind, q_scale in groups:
        c0 = head * HEAD_DIM
        acc = _dot(h, w_ref[:, c0:c0 + nheads * HEAD_DIM])
        for k in range(nheads):
            t = acc[:, k * HEAD_DIM:(k + 1) * HEAD_DIM]
            if gain_row is not None:
                t = _rmsnorm(t, gain_ref[gain_row:gain_row + 1, :])
            if use_rope and kind is not None:
                t = _rotate(t, rope_ref, kind)
            qkv_ref[head + k] = (t if q_scale is None else t * q_scale).astype(BF16)
            if cache_heads is not None:
                for dst_ref, heads in ((ck_ref, cache_heads[0]), (cv_ref, cache_heads[1])):
                    if head + k in heads:
                        slot = heads.index(head + k)
                        for s in range(tm // seq):
                            dst_ref[s, slot] = t[s * seq:(s + 1) * seq, :]
        head += nheads


def _proj_call(x, mods, w_in, gains, rope, li, seq=None, tm=512):
    nb, n, d = x.shape
    assert n % tm == 0 and (seq is None or tm % seq == 0)
    tiles = n // tm
    even = li % 2 == 0
    groups = _EVEN_GROUPS if even else _ODD_GROUPS
    use_rope = rope is not None
    cache_heads = None if use_rope else (_EVEN_CACHE if even else _ODD_CACHE)
    in_specs = [
        pl.BlockSpec((1, tm, d), lambda bb, i: (bb, i, 0)),
        pl.BlockSpec((1, N_MOD, d), lambda bb, i: (bb, 0, 0)),
        pl.BlockSpec((None, d, PROJ_W), lambda bb, i: (li, 0, 0), pipeline_mode=pl.Buffered(1)),
        pl.BlockSpec((2, HEAD_DIM), lambda bb, i: (0, 0)),
    ]
    args = [x, mods, w_in, gains]
    if use_rope:
        in_specs.append(pl.BlockSpec((6, tm, HEAD_DIM), lambda bb, i: (0, i, 0)))
        args.append(rope)
    out_shape = [jax.ShapeDtypeStruct((N_PROJ_HEADS, nb * n, HEAD_DIM), BF16)]
    out_specs = [pl.BlockSpec((N_PROJ_HEADS, tm, HEAD_DIM), lambda bb, i: (0, bb * tiles + i, 0))]
    if cache_heads is not None:
        per = tm // seq
        for _ in range(2):
            out_shape.append(jax.ShapeDtypeStruct((nb * n // seq, KV_HEADS, seq, HEAD_DIM), F32))
            out_specs.append(pl.BlockSpec((per, KV_HEADS, seq, HEAD_DIM), lambda bb, i: (bb * tiles + i, 0, 0, 0)))
    return pl.pallas_call(
        functools.partial(_proj_kernel, groups=groups, use_rope=use_rope, cache_heads=cache_heads, seq=seq),
        out_shape=out_shape,
        grid=(nb, tiles),
        in_specs=in_specs,
        out_specs=out_specs,
        compiler_params=_params("parallel", "parallel"),
        name="qkv_proj",
    )(*args)


def _softmax_step(state, s, v1):
    row_max = jnp.max(s, axis=-1, keepdims=True)
    if state is None:
        p = jnp.exp2(s - row_max)
        return row_max, _dot(p.astype(BF16), v1)
    m, acc = state
    m_new = jnp.maximum(m, row_max)
    alpha = jnp.exp2(m - m_new)
    p = jnp.exp2(s - m_new)
    return m_new, alpha * acc + _dot(p.astype(BF16), v1)


def _with_ones(v):
    return jnp.concatenate([v, jnp.ones(v.shape, v.dtype)], axis=1)


def _normalize(acc, extra=None):
    den = acc[:, HEAD_DIM:]
    if extra is not None:
        den = den + extra
    return acc[:, :HEAD_DIM] / den


def _sink_column(sink_ref, first, groups, rows_per_head):
    row = lax.broadcasted_iota(jnp.int32, (groups * rows_per_head, 1), 0)
    col = jnp.full((groups * rows_per_head, 1), sink_ref[first], F32)
    for g in range(1, groups):
        col = jnp.where(row >= g * rows_per_head, sink_ref[first + g], col)
    return col * LOG2E


def _dense_kernel(*refs, hps, groups, tq, nk, tk, has_ctx, has_sink):
    pos = 0
    sink_ref = None
    if has_sink:
        sink_ref = refs[0]
        pos = 1
    q_ref, k_ref, v_ref = refs[pos:pos + 3]
    pos += 3
    if has_ctx:
        kc_ref, vc_ref = refs[pos:pos + 2]
        pos += 2
    o_ref = refs[pos]

    for hh in range(hps):
        q = q_ref[hh * groups:(hh + 1) * groups].reshape(groups * tq, HEAD_DIM)
        state = None
        for c in range(nk // tk):
            s = _dot_nt(q, k_ref[hh, c * tk:(c + 1) * tk, :])
            state = _softmax_step(state, s, _with_ones(v_ref[hh, c * tk:(c + 1) * tk, :]))
        if has_ctx:
            s = _dot_nt(q, kc_ref[hh])
            state = _softmax_step(state, s, _with_ones(vc_ref[hh]))
        m, acc = state
        extra = None
        if has_sink:
            sink = _sink_column(sink_ref, (pl.program_id(1) * hps + hh) * groups, groups, tq)
            m_new = jnp.maximum(m, sink)
            acc = jnp.exp2(m - m_new) * acc
            extra = jnp.exp2(sink - m_new)
        o = _normalize(acc, extra)
        for g in range(groups):
            col = (hh * groups + g) * HEAD_DIM
            o_ref[:, col:col + HEAD_DIM] = o[g * tq:(g + 1) * tq, :].astype(o_ref.dtype)


def _dense_call(qkv, q_head0, k_head0, v_head0, n_kv, groups, nb, n, tq, tk, hps=1, ctx=None, sink=None):
    qt = n // tq
    assert n_kv % hps == 0 and q_head0 % (hps * groups) == 0 and k_head0 % hps == 0 and v_head0 % hps == 0
    in_specs, args = [], []
    if sink is not None:
        in_specs.append(pl.BlockSpec(memory_space=pltpu.SMEM))
        args.append(sink)
    qh0, kh0, vh0 = q_head0 // (hps * groups), k_head0 // hps, v_head0 // hps
    in_specs += [
        pl.BlockSpec((hps * groups, tq, HEAD_DIM), lambda bb, h, i: (qh0 + h, bb * qt + i, 0)),
        pl.BlockSpec((hps, n, HEAD_DIM), lambda bb, h, i: (kh0 + h, bb, 0)),
        pl.BlockSpec((hps, n, HEAD_DIM), lambda bb, h, i: (vh0 + h, bb, 0)),
    ]
    args += [qkv, qkv, qkv]
    if ctx is not None:
        ck, cv, li, ch0 = ctx
        assert ch0 % hps == 0
        m = ck.shape[3]
        for arr in (ck, cv):
            in_specs.append(pl.BlockSpec((None, None, hps, m, HEAD_DIM),
                                         lambda bb, h, i: (bb, li, ch0 // hps + h, 0, 0)))
            args.append(arr)
    kern = functools.partial(_dense_kernel, hps=hps, groups=groups, tq=tq, nk=n, tk=tk, has_ctx=ctx is not None,
                             has_sink=sink is not None)
    return pl.pallas_call(
        kern,
        out_shape=jax.ShapeDtypeStruct((nb * n, n_kv * groups * HEAD_DIM), BF16),
        grid=(nb, n_kv // hps, qt),
        in_specs=in_specs,
        out_specs=pl.BlockSpec((tq, hps * groups * HEAD_DIM), lambda bb, h, i: (bb * qt + i, h)),
        compiler_params=_params("parallel", "parallel", "parallel"),
        name="dense_attn",
    )(*args)


def _diff_kernel(*refs, hps, nk, tk, has_ctx, lam_init):
    q_ref, k_ref, v_ref = refs[:3]
    pos = 3
    if has_ctx:
        kc_ref, vc_ref = refs[pos:pos + 2]
        pos += 2
    lam_ref, gain_ref, o_ref = refs[pos:pos + 3]

    lb = lam_ref[...]
    lam = (jnp.exp(jnp.sum(lb[0:1] * lb[1:2], axis=-1, keepdims=True))
           - jnp.exp(jnp.sum(lb[2:3] * lb[3:4], axis=-1, keepdims=True)) + lam_init)
    for hh in range(hps):
        q = q_ref[hh]
        lane = lax.broadcasted_iota(jnp.int32, q.shape, 1)
        zero = jnp.zeros_like(q)
        q1 = jnp.where(lane < DIFF_HALF, q, zero)
        q2 = jnp.where(lane >= DIFF_HALF, q, zero)
        st1 = st2 = None
        for c in range(nk // tk):
            kc = k_ref[hh, c * tk:(c + 1) * tk, :]
            vc = _with_ones(v_ref[hh, c * tk:(c + 1) * tk, :])
            st1 = _softmax_step(st1, _dot_nt(q1, kc), vc)
            st2 = _softmax_step(st2, _dot_nt(q2, kc), vc)
        if has_ctx:
            vc = _with_ones(vc_ref[hh])
            st1 = _softmax_step(st1, _dot_nt(q1, kc_ref[hh]), vc)
            st2 = _softmax_step(st2, _dot_nt(q2, kc_ref[hh]), vc)
        o = _normalize(st1[1]) - lam * _normalize(st2[1])
        o = _rmsnorm(o, gain_ref[...]) * (1.0 - lam_init)
        o_ref[:, hh * HEAD_DIM:(hh + 1) * HEAD_DIM] = o.astype(o_ref.dtype)


def _diff_call(qkv, lam_b, subln, li, nb, n, tq, tk, hps=1, ctx=None):
    qt = n // tq
    e = li // 2
    lam_init = 0.8 - 0.6 * math.exp(-0.3 * li)
    assert H_B % hps == 0 and QB % hps == 0 and (ctx is None or HKV_A % hps == 0)
    in_specs = [
        pl.BlockSpec((hps, tq, HEAD_DIM), lambda bb, h, i: (QB // hps + h, bb * qt + i, 0)),
        pl.BlockSpec((hps, n, HEAD_DIM), lambda bb, h, i: (KB // hps + h, bb, 0)),
        pl.BlockSpec((hps, n, HEAD_DIM), lambda bb, h, i: (VB // hps + h, bb, 0)),
    ]
    args = [qkv, qkv, qkv]
    if ctx is not None:
        ck, cv = ctx
        m = ck.shape[3]
        for arr in (ck, cv):
            in_specs.append(pl.BlockSpec((None, None, hps, m, HEAD_DIM),
                                         lambda bb, h, i: (bb, li, HKV_A // hps + h, 0, 0)))
            args.append(arr)
    in_specs += [
        pl.BlockSpec((None, 4, DIFF_HALF), lambda bb, h, i: (e, 0, 0)),
        pl.BlockSpec((None, 1, HEAD_DIM), lambda bb, h, i: (e, 0, 0)),
    ]
    args += [lam_b, subln.reshape(-1, 1, HEAD_DIM)]
    return pl.pallas_call(
        functools.partial(_diff_kernel, hps=hps, nk=n, tk=tk, has_ctx=ctx is not None, lam_init=lam_init),
        out_shape=jax.ShapeDtypeStruct((nb * n, H_B * HEAD_DIM), BF16),
        grid=(nb, H_B // hps, qt),
        in_specs=in_specs,
        out_specs=pl.BlockSpec((tq, hps * HEAD_DIM), lambda bb, h, i: (bb * qt + i, h)),
        compiler_params=_params("parallel", "parallel", "parallel"),
        name="diff_attn",
    )(*args)


NA_Q_ROWS = 4
NA_K_ROWS = NA_Q_ROWS + NA_ROWS


def _na_window_start(j, rows):
    return jnp.clip(NA_Q_ROWS * j - NA_ROWS // 2, 0, rows - NA_K_ROWS)


def _na_bias(rpb, rows):
    nj = rows // NA_Q_ROWS
    heads = rpb.shape[0]
    n_ro, n_co = 2 * NA_ROWS - 1, 2 * NA_COLS - 1
    c = np.arange(GRID_W)[:, None]
    kc = np.arange(GRID_W)[None, :]
    cs = np.clip(c - NA_COLS // 2, 0, GRID_W - NA_COLS)
    col_valid = (kc >= cs) & (kc < cs + NA_COLS)
    onehot = (kc - c + NA_COLS - 1)[None] == np.arange(n_co)[:, None, None]
    rpb2 = rpb.astype(F32) * LOG2E
    band = jnp.sum(jnp.where(onehot[None, None], rpb2[:, :, :, None, None], 0.0), axis=2)
    band = jnp.where(col_valid[None, None], band, MASK_VALUE)
    blocks = jnp.concatenate([band, jnp.full((heads, 1, GRID_W, GRID_W), MASK_VALUE, F32)], axis=1)
    pick = np.full((3, NA_Q_ROWS, NA_K_ROWS), n_ro, np.int32)
    for v, j in enumerate((0, 1, nj - 1)):
        ws = int(np.clip(NA_Q_ROWS * j - NA_ROWS // 2, 0, rows - NA_K_ROWS))
        for rq in range(NA_Q_ROWS):
            r = NA_Q_ROWS * j + rq
            rs = int(np.clip(r - NA_ROWS // 2, 0, rows - NA_ROWS))
            for kr in range(NA_K_ROWS):
                if rs <= ws + kr < rs + NA_ROWS:
                    pick[v, rq, kr] = ws + kr - r + NA_ROWS - 1
    rows_of_blocks = []
    for v in range(3):
        for rq in range(NA_Q_ROWS):
            rows_of_blocks.append(jnp.concatenate([blocks[:, int(pick[v, rq, kr])] for kr in range(NA_K_ROWS)], axis=-1))
    table = jnp.stack(rows_of_blocks, axis=1)
    return table.reshape(heads, 3, NA_Q_ROWS * GRID_W, NA_K_ROWS * GRID_W)


def _windowed_softmax(q, k_loc, v_loc, k_ctx, v_ctx, bias=None, mask=None, sink=None):
    s_loc = _dot_nt(q, k_loc)
    if bias is not None:
        s_loc = s_loc + bias
    if mask is not None:
        s_loc = jnp.where(mask, s_loc, MASK_VALUE)
    s_ctx = _dot_nt(q, k_ctx)
    m = jnp.maximum(jnp.max(s_loc, axis=-1, keepdims=True), jnp.max(s_ctx, axis=-1, keepdims=True))
    if sink is not None:
        m = jnp.maximum(m, sink)
    p_loc = jnp.exp2(s_loc - m)
    p_ctx = jnp.exp2(s_ctx - m)
    l = jnp.sum(p_loc, axis=-1, keepdims=True) + jnp.sum(p_ctx, axis=-1, keepdims=True)
    if sink is not None:
        l = l + jnp.exp2(sink - m)
    return (_dot(p_loc.astype(BF16), v_loc) + _dot(p_ctx.astype(BF16), v_ctx)) / l


def _na_kernel(q_ref, k_ref, v_ref, kc_ref, vc_ref, bias_ref, o_ref):
    for h in range(H_C):
        o = _windowed_softmax(q_ref[h], k_ref[h], v_ref[h], kc_ref[h], vc_ref[h], bias=bias_ref[h])
        o_ref[:, h * HEAD_DIM:(h + 1) * HEAD_DIM] = o.astype(o_ref.dtype)


def _na_call(qkv, bias, ck, cv, li, nb, n):
    rows = n // GRID_W
    nj = rows // NA_Q_ROWS
    tq = NA_Q_ROWS * GRID_W
    span = NA_K_ROWS * GRID_W
    m = ck.shape[3]

    window = (pl.Element(H_C), pl.Element(span), pl.Element(HEAD_DIM))

    def window_map(first_head):
        return lambda bb, j: (first_head, pl.multiple_of(bb * n + _na_window_start(j, rows) * GRID_W, GRID_W), 0)

    def bias_map(bb, j):
        return (0, jnp.where(j == 0, 0, jnp.where(j == nj - 1, 2, 1)), 0, 0)

    return pl.pallas_call(
        _na_kernel,
        out_shape=jax.ShapeDtypeStruct((nb * n, H_C * HEAD_DIM), BF16),
        grid=(nb, nj),
        in_specs=[
            pl.BlockSpec((H_C, tq, HEAD_DIM), lambda bb, j: (QC // H_C, bb * nj + j, 0)),
            pl.BlockSpec(window, window_map(KC)),
            pl.BlockSpec(window, window_map(VC)),
            pl.BlockSpec((None, None, H_C, m, HEAD_DIM), lambda bb, j: (bb, li, 0, 0, 0)),
            pl.BlockSpec((None, None, H_C, m, HEAD_DIM), lambda bb, j: (bb, li, 0, 0, 0)),
            pl.BlockSpec((H_C, None, tq, span), bias_map),
        ],
        out_specs=pl.BlockSpec((tq, H_C * HEAD_DIM), lambda bb, j: (bb * nj + j, 0)),
        compiler_params=_params("parallel", "parallel"),
        name="na2d_attn",
    )(qkv, qkv, qkv, ck, cv, bias)


BAND_SPAN = 3 * WINDOW
BAND_GROUPS = H_D // HKV_D


BAND_BLOCKS_PER_STEP = 4


def _band_kernel(sink_ref, q_ref, k_ref, v_ref, kc_ref, vc_ref, o_ref, *, n):
    rows = BAND_GROUPS * WINDOW
    for s in range(BAND_BLOCKS_PER_STEP):
        blk = pl.program_id(1) * BAND_BLOCKS_PER_STEP + s
        start = pl.multiple_of(jnp.clip((blk - 1) * WINDOW, 0, n - BAND_SPAN), WINDOW)
        qpos = blk * WINDOW + lax.rem(lax.broadcasted_iota(jnp.int32, (rows, BAND_SPAN), 0), WINDOW)
        kpos = start + lax.broadcasted_iota(jnp.int32, (rows, BAND_SPAN), 1)
        in_band = jnp.abs(qpos - kpos) <= WINDOW
        tok = slice(s * WINDOW, (s + 1) * WINDOW)
        for h in range(HKV_D):
            q = q_ref[h * BAND_GROUPS:(h + 1) * BAND_GROUPS, tok, :].reshape(rows, HEAD_DIM)
            sink = _sink_column(sink_ref, h * BAND_GROUPS, BAND_GROUPS, WINDOW)
            o = _windowed_softmax(q, k_ref[h, pl.ds(start, BAND_SPAN), :], v_ref[h, pl.ds(start, BAND_SPAN), :],
                                  kc_ref[h], vc_ref[h], mask=in_band, sink=sink)
            for g in range(BAND_GROUPS):
                col = (h * BAND_GROUPS + g) * HEAD_DIM
                o_ref[tok, col:col + HEAD_DIM] = o[g * WINDOW:(g + 1) * WINDOW, :].astype(o_ref.dtype)


def _band_call(qkv, sink, ck, cv, li, nb, n):
    tq = BAND_BLOCKS_PER_STEP * WINDOW
    nblk = n // tq
    m = ck.shape[3]
    return pl.pallas_call(
        functools.partial(_band_kernel, n=n),
        out_shape=jax.ShapeDtypeStruct((nb * n, H_D * HEAD_DIM), BF16),
        grid=(nb, nblk),
        in_specs=[
            pl.BlockSpec(memory_space=pltpu.SMEM),
            pl.BlockSpec((H_D, tq, HEAD_DIM), lambda bb, i: (QD // H_D, bb * nblk + i, 0)),
            pl.BlockSpec((HKV_D, n, HEAD_DIM), lambda bb, i: (KD // HKV_D, bb, 0)),
            pl.BlockSpec((HKV_D, n, HEAD_DIM), lambda bb, i: (VD // HKV_D, bb, 0)),
            pl.BlockSpec((None, None, HKV_D, m, HEAD_DIM), lambda bb, i: (bb, li, H_C // HKV_D, 0, 0)),
            pl.BlockSpec((None, None, HKV_D, m, HEAD_DIM), lambda bb, i: (bb, li, H_C // HKV_D, 0, 0)),
        ],
        out_specs=pl.BlockSpec((tq, H_D * HEAD_DIM), lambda bb, i: (bb * nblk + i, 0)),
        compiler_params=_params("parallel", "parallel"),
        name="band_attn",
    )(sink, qkv, qkv, qkv, ck, cv)


def _outproj_kernel(x_ref, mod_ref, ma_ref, mb_ref, w_ref, g_ref, b_ref, o_ref):
    gate = mod_ref[0, 5:6, :]
    for r0 in range(0, o_ref.shape[1], OUTPROJ_ROW_SLAB):
        rows = slice(r0, r0 + OUTPROJ_ROW_SLAB)
        f = _dot(ma_ref[rows, :], w_ref[:MIX_HALF, :]) + _dot(mb_ref[rows, :], w_ref[MIX_HALF:, :])
        o_ref[0, rows, :] = _layernorm(DN_ALPHA * x_ref[0, rows, :] + gate * f, g_ref[...], b_ref[...])


def _outproj_call(x, mods, mix_a, mix_b, w_out, ln_g, ln_b, li, tm=1024):
    nb, n, d = x.shape
    tm = min(tm, n)
    assert n % tm == 0 and tm % OUTPROJ_ROW_SLAB == 0
    tiles = n // tm
    return pl.pallas_call(
        _outproj_kernel,
        out_shape=jax.ShapeDtypeStruct(x.shape, F32),
        grid=(nb, tiles),
        in_specs=[
            pl.BlockSpec((1, tm, d), lambda bb, i: (bb, i, 0)),
            pl.BlockSpec((1, N_MOD, d), lambda bb, i: (bb, 0, 0)),
            pl.BlockSpec((tm, MIX_HALF), lambda bb, i: (bb * tiles + i, 0)),
            pl.BlockSpec((tm, MIX_HALF), lambda bb, i: (bb * tiles + i, 0)),
            pl.BlockSpec((None, 2 * MIX_HALF, d), lambda bb, i: (li, 0, 0), pipeline_mode=pl.Buffered(1)),
            pl.BlockSpec((1, d), lambda bb, i: (0, 0)),
            pl.BlockSpec((1, d), lambda bb, i: (0, 0)),
        ],
        out_specs=pl.BlockSpec((1, tm, d), lambda bb, i: (bb, i, 0)),
        compiler_params=_params("parallel", "parallel"),
        name="out_proj",
    )(x, mods, mix_a, mix_b, w_out, ln_g[li, 1][None, :], ln_b[li, 1][None, :])


def _rope_tables(n):
    t = jnp.arange(n, dtype=jnp.int32)
    rowcol = jnp.stack([t // GRID_W, t % GRID_W], axis=-1).astype(F32)
    lane = np.arange(HEAD_DIM)
    out = []
    for half in (32, 16):
        which = (lane // (2 * half)) % 2
        idx = lane % half
        low = (lane % (2 * half)) < half
        freqs = ROPE_THETA ** (-jnp.arange(half, dtype=F32) / half)
        ang = rowcol[:, which] * freqs[idx][None, :]
        cos, sin = jnp.cos(ang), jnp.sin(ang)
        out += [cos, jnp.where(low[None, :], -sin, 0.0), jnp.where(low[None, :], 0.0, sin)]
    return jnp.stack(out, axis=0)


def kernel(x_prompt, x_sample, cache_k, cache_v, c, c_ctx, w_ada, b_ada, ln_g, ln_b, ffn_w1, ffn_w3, ffn_w2,
           w_in, w_out, qk_gain_a, lam_b, subln_b, rpb_c, sink_d):
    batch, seq, d = x_prompt.shape
    dec_batch, dec_seq, _ = x_sample.shape

    w1, w3, w2 = ffn_w1.astype(BF16), ffn_w3.astype(BF16), ffn_w2.astype(BF16)
    w_in_b, w_out_b = w_in.astype(BF16), w_out.astype(BF16)
    ck_b, cv_b = cache_k.astype(BF16), cache_v.astype(BF16)

    assert dec_batch + 1 <= COND_ROWS
    cond = jnp.zeros((COND_ROWS, d), F32).at[:dec_batch].set(c).at[dec_batch].set(c_ctx)
    mods = _mods_call(cond, w_ada, b_ada).reshape(DEPTH, COND_ROWS, N_MOD, d)

    rope = _rope_tables(dec_seq)
    rows = dec_seq // GRID_W
    no_gain = jnp.ones((2, HEAD_DIM), F32)
    ctx_diff_hps = math.gcd(H_B, QB)

    xp = x_prompt.reshape(1, batch * seq, d)
    xs = x_sample
    new_k, new_v = [], []
    for li in range(DEPTH):
        even = li % 2 == 0
        e = li // 2
        mods_s = mods[li, :dec_batch]
        mods_p = mods[li, dec_batch:dec_batch + 1]
        gains = qk_gain_a[e] if even else no_gain

        xp = _ffn_call(xp, mods_p, w1, w3, w2, ln_g, ln_b, li, 0)
        qkv, ck_new, cv_new = _proj_call(xp, mods_p, w_in_b, gains, None, li, seq=seq)
        new_k.append(ck_new)
        new_v.append(cv_new)
        if even:
            mix_a = _dense_call(qkv, QA, KA, VA, HKV_A, H_A // HKV_A, batch, seq, seq, seq, hps=HKV_A)
            mix_b = _diff_call(qkv, lam_b, subln_b, li, batch, seq, seq, seq, hps=ctx_diff_hps)
        else:
            mix_a = _dense_call(qkv, QC, KC, VC, H_C, 1, batch, seq, seq, seq, hps=H_C)
            mix_b = _dense_call(qkv, QD, KD, VD, HKV_D, H_D // HKV_D, batch, seq, seq, seq, hps=HKV_D,
                                sink=sink_d[e])
        xp = _outproj_call(xp, mods_p, mix_a, mix_b, w_out_b, ln_g, ln_b, li)
        xp = _ffn_call(xp, mods_p, w1, w3, w2, ln_g, ln_b, li, 1)

        xs = _ffn_call(xs, mods_s, w1, w3, w2, ln_g, ln_b, li, 0)
        (qkv,) = _proj_call(xs, mods_s, w_in_b, gains, rope, li)
        if even:
            mix_a = _dense_call(qkv, QA, KA, VA, HKV_A, H_A // HKV_A, dec_batch, dec_seq, *LATENT_DENSE_TILES,
                                ctx=(ck_b, cv_b, li, 0))
            mix_b = _diff_call(qkv, lam_b, subln_b, li, dec_batch, dec_seq, *LATENT_DIFF_TILES, ctx=(ck_b, cv_b))
        else:
            mix_a = _na_call(qkv, _na_bias(rpb_c[e], rows), ck_b, cv_b, li, dec_batch, dec_seq)
            mix_b = _band_call(qkv, sink_d[e], ck_b, cv_b, li, dec_batch, dec_seq)
        xs = _outproj_call(xs, mods_s, mix_a, mix_b, w_out_b, ln_g, ln_b, li)
        xs = _ffn_call(xs, mods_s, w1, w3, w2, ln_g, ln_b, li, 1)

    y_prompt = xp.reshape(batch, seq, d)
    new_cache_k = jnp.stack(new_k, axis=1)
    new_cache_v = jnp.stack(new_v, axis=1)
    return (y_prompt, xs, new_cache_k, new_cache_v)
```

```python
import functools
import math

import jax
import jax.numpy as jnp
import numpy as np
from jax import lax
from jax.experimental import pallas as pl
from jax.experimental.pallas import tpu as pltpu

D_MODEL = 2048
DEPTH = 4
GRID_W = 64
HEAD_DIM = 128
DIFF_HALF = HEAD_DIM // 2
H_A = 8
HKV_A = 2
H_B = 8
H_C = 8
H_D = 8
HKV_D = 2
KV_HEADS = HKV_A + H_B
NA_ROWS = 8
NA_COLS = 16
WINDOW = 128
ROPE_THETA = 10000.0
LN_EPS = 1e-6
N_MOD = 9
DN_ALPHA = (2 * DEPTH) ** 0.25
PROJ_W = H_A * HEAD_DIM + 2 * HKV_A * HEAD_DIM + 3 * H_B * HEAD_DIM
N_PROJ_HEADS = PROJ_W // HEAD_DIM
MIX_HALF = H_A * HEAD_DIM

V7X_VMEM_BYTES = 64 * 1024 * 1024
VMEM_LIMIT_BYTES = V7X_VMEM_BYTES - 8 * 1024 * 1024
FFN_VMEM_LIMIT_BYTES = V7X_VMEM_BYTES - 3 * 1024 * 1024
LANES = 128
MASK_VALUE = -1e30
LOG2E = math.log2(math.e)
FFN_COL_CHUNK = 512
FFN_ROW_SLAB = 512
OUTPROJ_ROW_SLAB = 256
COND_ROWS = 16
LATENT_DENSE_TILES = (256, 1024)
LATENT_DIFF_TILES = (1024, 512)

BF16 = jnp.bfloat16
F32 = jnp.float32


def _params(*semantics, vmem_limit=VMEM_LIMIT_BYTES):
    return pltpu.CompilerParams(dimension_semantics=semantics, vmem_limit_bytes=vmem_limit)


def _dot(a, b):
    return jnp.dot(a, b, preferred_element_type=F32)


def _dot_nt(a, b):
    return lax.dot_general(a, b, (((1,), (1,)), ((), ())), preferred_element_type=F32)


def _layernorm(y, g, b):
    mu = jnp.mean(y, axis=-1, keepdims=True)
    yc = y - mu
    var = jnp.mean(yc * yc, axis=-1, keepdims=True)
    return yc * lax.rsqrt(var + LN_EPS) * g + b


def _rmsnorm(t, g):
    return t * lax.rsqrt(jnp.mean(t * t, axis=-1, keepdims=True) + LN_EPS) * g


def _mods_kernel(c_ref, w_ref, b_ref, o_ref):
    c = c_ref[...]
    s = (c / (1.0 + jnp.exp(-c))).astype(BF16)
    o_ref[...] = _dot(s, w_ref[...].astype(BF16)) + b_ref[...]


def _mods_call(cond, w_ada, b_ada, tn=1024):
    rows = cond.shape[0]
    width = N_MOD * D_MODEL
    return pl.pallas_call(
        _mods_kernel,
        out_shape=jax.ShapeDtypeStruct((DEPTH, rows, width), F32),
        grid=(DEPTH, width // tn),
        in_specs=[
            pl.BlockSpec((rows, D_MODEL), lambda l, j: (0, 0)),
            pl.BlockSpec((None, D_MODEL, tn), lambda l, j: (l, 0, j)),
            pl.BlockSpec((None, 1, tn), lambda l, j: (l, 0, j)),
        ],
        out_specs=pl.BlockSpec((None, rows, tn), lambda l, j: (l, 0, j)),
        compiler_params=_params("parallel", "parallel"),
        name="adaln_mods",
    )(cond, w_ada, b_ada.reshape(DEPTH, 1, width))


def _ffn_kernel(x_ref, mod_ref, w1_ref, w3_ref, w2_ref, g_ref, b_ref, o_ref, h_ref, *, mod_base):
    j = pl.program_id(2)
    last = pl.num_programs(2) - 1

    def step(first, final):
        slab = FFN_ROW_SLAB // 2 if final else FFN_ROW_SLAB
        for r0 in range(0, o_ref.shape[1], slab):
            rows = slice(r0, r0 + slab)
            if first:
                shift = mod_ref[0, mod_base:mod_base + 1, :]
                scale = mod_ref[0, mod_base + 1:mod_base + 2, :]
                h = (x_ref[0, rows, :] * (1.0 + scale) + shift).astype(BF16)
                h_ref[rows, :] = h
            else:
                h = h_ref[rows, :]
            a = _dot(h, w1_ref[...])
            b = _dot(h, w3_ref[...])
            u = (a / (1.0 + jnp.exp(-a)) * b).astype(BF16)
            for c0 in range(0, o_ref.shape[2], FFN_COL_CHUNK):
                cols = slice(c0, c0 + FFN_COL_CHUNK)
                part = _dot(u, w2_ref[:, cols])
                if first:
                    o_ref[0, rows, cols] = part
                else:
                    o_ref[0, rows, cols] += part
            if final:
                gate = mod_ref[0, mod_base + 2:mod_base + 3, :]
                y = DN_ALPHA * x_ref[0, rows, :] + (0.5 * gate) * o_ref[0, rows, :]
                o_ref[0, rows, :] = _layernorm(y, g_ref[...], b_ref[...])

    pl.when(j == 0)(functools.partial(step, True, False))
    pl.when(jnp.logical_and(j > 0, j < last))(functools.partial(step, False, False))
    pl.when(j == last)(functools.partial(step, False, True))


def _ffn_call(x, mods, w1, w3, w2, ln_g, ln_b, li, slot, tm=1024, tf=512):
    nb, n, d = x.shape
    tm = min(tm, n)
    assert n % tm == 0 and tm % FFN_ROW_SLAB == 0 and w1.shape[-1] // tf >= 2
    mod_base = 0 if slot == 0 else 6
    ln_idx = 0 if slot == 0 else 2
    g = ln_g[li, ln_idx][None, :]
    b = ln_b[li, ln_idx][None, :]
    return pl.pallas_call(
        functools.partial(_ffn_kernel, mod_base=mod_base),
        out_shape=jax.ShapeDtypeStruct(x.shape, F32),
        grid=(nb, n // tm, w1.shape[-1] // tf),
        in_specs=[
            pl.BlockSpec((1, tm, d), lambda bb, i, j: (bb, i, 0)),
            pl.BlockSpec((1, N_MOD, d), lambda bb, i, j: (bb, 0, 0)),
            pl.BlockSpec((None, None, d, tf), lambda bb, i, j: (li, slot, 0, j)),
            pl.BlockSpec((None, None, d, tf), lambda bb, i, j: (li, slot, 0, j)),
            pl.BlockSpec((None, None, tf, d), lambda bb, i, j: (li, slot, j, 0)),
            pl.BlockSpec((1, d), lambda bb, i, j: (0, 0)),
            pl.BlockSpec((1, d), lambda bb, i, j: (0, 0)),
        ],
        out_specs=pl.BlockSpec((1, tm, d), lambda bb, i, j: (bb, i, 0)),
        scratch_shapes=[pltpu.VMEM((tm, d), BF16)],
        compiler_params=_params("parallel", "parallel", "arbitrary", vmem_limit=FFN_VMEM_LIMIT_BYTES),
        name="ffn",
    )(x, mods, w1, w3, w2, g, b)


Q_SCALE = HEAD_DIM ** -0.5 * LOG2E
Q_SCALE_DIFF = DIFF_HALF ** -0.5 * LOG2E
_EVEN_GROUPS = ((H_A, 0, "axial", Q_SCALE), (HKV_A, 1, "axial", None), (HKV_A, None, None, None),
                (H_B, None, "diff", Q_SCALE_DIFF), (H_B, None, "diff", None), (H_B, None, None, None))
_ODD_GROUPS = ((H_C, None, None, Q_SCALE), (H_C, None, None, None), (H_C, None, None, None),
               (H_D, None, "axial", Q_SCALE), (HKV_D, None, "axial", None), (HKV_D, None, None, None))


def _first_heads(groups):
    firsts, head = [], 0
    for group in groups:
        firsts.append(head)
        head += group[0]
    return tuple(firsts)


QA, KA, VA, QB, KB, VB = _first_heads(_EVEN_GROUPS)
QC, KC, VC, QD, KD, VD = _first_heads(_ODD_GROUPS)
_EVEN_CACHE = (tuple(range(KA, KA + HKV_A)) + tuple(range(KB, KB + H_B)),
               tuple(range(VA, VA + HKV_A)) + tuple(range(VB, VB + H_B)))
_ODD_CACHE = (tuple(range(KC, KC + H_C)) + tuple(range(KD, KD + HKV_D)),
              tuple(range(VC, VC + H_C)) + tuple(range(VD, VD + HKV_D)))


def _rotate(t, rope_ref, kind):
    base, half = (0, 32) if kind == "axial" else (3, 16)
    up = pltpu.roll(t, LANES - half, 1)
    down = pltpu.roll(t, half, 1)
    return t * rope_ref[base] + up * rope_ref[base + 1] + down * rope_ref[base + 2]


def _proj_kernel(*refs, groups, use_rope, cache_heads, seq):
    x_ref, mod_ref, w_ref, gain_ref = refs[:4]
    pos = 4
    rope_ref = None
    if use_rope:
        rope_ref = refs[pos]
        pos += 1
    qkv_ref = refs[pos]
    pos += 1
    ck_ref = cv_ref = None
    if cache_heads is not None:
        ck_ref, cv_ref = refs[pos], refs[pos + 1]

    shift = mod_ref[0, 3:4, :]
    scale = mod_ref[0, 4:5, :]
    h = (x_ref[0] * (1.0 + scale) + shift).astype(BF16)
    tm = h.shape[0]
    head = 0
    for nheads, gain_row, kind, q_scale in groups:
        c0 = head * HEAD_DIM
        acc = _dot(h, w_ref[:, c0:c0 + nheads * HEAD_DIM])
        for k in range(nheads):
            t = acc[:, k * HEAD_DIM:(k + 1) * HEAD_DIM]
            if gain_row is not None:
                t = _rmsnorm(t, gain_ref[gain_row:gain_row + 1, :])
            if use_rope and kind is not None:
                t = _rotate(t, rope_ref, kind)
            qkv_ref[head + k] = (t if q_scale is None else t * q_scale).astype(BF16)
            if cache_heads is not None:
                for dst_ref, heads in ((ck_ref, cache_heads[0]), (cv_ref, cache_heads[1])):
                    if head + k in heads:
                        slot = heads.index(head + k)
                        for s in range(tm // seq):
                            dst_ref[s, slot] = t[s * seq:(s + 1) * seq, :]
        head += nheads


def _proj_call(x, mods, w_in, gains, rope, li, seq=None, tm=512):
    nb, n, d = x.shape
    assert n % tm == 0 and (seq is None or tm % seq == 0)
    tiles = n // tm
    even = li % 2 == 0
    groups = _EVEN_GROUPS if even else _ODD_GROUPS
    use_rope = rope is not None
    cache_heads = None if use_rope else (_EVEN_CACHE if even else _ODD_CACHE)
    in_specs = [
        pl.BlockSpec((1, tm, d), lambda bb, i: (bb, i, 0)),
        pl.BlockSpec((1, N_MOD, d), lambda bb, i: (bb, 0, 0)),
        pl.BlockSpec((None, d, PROJ_W), lambda bb, i: (li, 0, 0), pipeline_mode=pl.Buffered(1)),
        pl.BlockSpec((2, HEAD_DIM), lambda bb, i: (0, 0)),
    ]
    args = [x, mods, w_in, gains]
    if use_rope:
        in_specs.append(pl.BlockSpec((6, tm, HEAD_DIM), lambda bb, i: (0, i, 0)))
        args.append(rope)
    out_shape = [jax.ShapeDtypeStruct((N_PROJ_HEADS, nb * n, HEAD_DIM), BF16)]
    out_specs = [pl.BlockSpec((N_PROJ_HEADS, tm, HEAD_DIM), lambda bb, i: (0, bb * tiles + i, 0))]
    if cache_heads is not None:
        per = tm // seq
        for _ in range(2):
            out_shape.append(jax.ShapeDtypeStruct((nb * n // seq, KV_HEADS, seq, HEAD_DIM), F32))
            out_specs.append(pl.BlockSpec((per, KV_HEADS, seq, HEAD_DIM), lambda bb, i: (bb * tiles + i, 0, 0, 0)))
    return pl.pallas_call(
        functools.partial(_proj_kernel, groups=groups, use_rope=use_rope, cache_heads=cache_heads, seq=seq),
        out_shape=out_shape,
        grid=(nb, tiles),
        in_specs=in_specs,
        out_specs=out_specs,
        compiler_params=_params("parallel", "parallel"),
        name="qkv_proj",
    )(*args)


def _softmax_step(state, s, v1):
    row_max = jnp.max(s, axis=-1, keepdims=True)
    if state is None:
        p = jnp.exp2(s - row_max)
        return row_max, _dot(p.astype(BF16), v1)
    m, acc = state
    m_new = jnp.maximum(m, row_max)
    alpha = jnp.exp2(m - m_new)
    p = jnp.exp2(s - m_new)
    return m_new, alpha * acc + _dot(p.astype(BF16), v1)


def _with_ones(v):
    return jnp.concatenate([v, jnp.ones(v.shape, v.dtype)], axis=1)


def _normalize(acc, extra=None):
    den = acc[:, HEAD_DIM:]
    if extra is not None:
        den = den + extra
    return acc[:, :HEAD_DIM] / den


def _sink_column(sink_ref, first, groups, rows_per_head):
    row = lax.broadcasted_iota(jnp.int32, (groups * rows_per_head, 1), 0)
    col = jnp.full((groups * rows_per_head, 1), sink_ref[first], F32)
    for g in range(1, groups):
        col = jnp.where(row >= g * rows_per_head, sink_ref[first + g], col)
    return col * LOG2E


def _dense_kernel(*refs, hps, groups, tq, nk, tk, has_ctx, has_sink):
    pos = 0
    sink_ref = None
    if has_sink:
        sink_ref = refs[0]
        pos = 1
    q_ref, k_ref, v_ref = refs[pos:pos + 3]
    pos += 3
    if has_ctx:
        kc_ref, vc_ref = refs[pos:pos + 2]
        pos += 2
    o_ref = refs[pos]

    for hh in range(hps):
        q = q_ref[hh * groups:(hh + 1) * groups].reshape(groups * tq, HEAD_DIM)
        state = None
        for c in range(nk // tk):
            s = _dot_nt(q, k_ref[hh, c * tk:(c + 1) * tk, :])
            state = _softmax_step(state, s, _with_ones(v_ref[hh, c * tk:(c + 1) * tk, :]))
        if has_ctx:
            s = _dot_nt(q, kc_ref[hh])
            state = _softmax_step(state, s, _with_ones(vc_ref[hh]))
        m, acc = state
        extra = None
        if has_sink:
            sink = _sink_column(sink_ref, (pl.program_id(1) * hps + hh) * groups, groups, tq)
            m_new = jnp.maximum(m, sink)
            acc = jnp.exp2(m - m_new) * acc
            extra = jnp.exp2(sink - m_new)
        o = _normalize(acc, extra)
        for g in range(groups):
            col = (hh * groups + g) * HEAD_DIM
            o_ref[:, col:col + HEAD_DIM] = o[g * tq:(g + 1) * tq, :].astype(o_ref.dtype)


def _dense_call(qkv, q_head0, k_head0, v_head0, n_kv, groups, nb, n, tq, tk, hps=1, ctx=None, sink=None):
    qt = n // tq
    assert n_kv % hps == 0 and q_head0 % (hps * groups) == 0 and k_head0 % hps == 0 and v_head0 % hps == 0
    in_specs, args = [], []
    if sink is not None:
        in_specs.append(pl.BlockSpec(memory_space=pltpu.SMEM))
        args.append(sink)
    qh0, kh0, vh0 = q_head0 // (hps * groups), k_head0 // hps, v_head0 // hps
    in_specs += [
        pl.BlockSpec((hps * groups, tq, HEAD_DIM), lambda bb, h, i: (qh0 + h, bb * qt + i, 0)),
        pl.BlockSpec((hps, n, HEAD_DIM), lambda bb, h, i: (kh0 + h, bb, 0)),
        pl.BlockSpec((hps, n, HEAD_DIM), lambda bb, h, i: (vh0 + h, bb, 0)),
    ]
    args += [qkv, qkv, qkv]
    if ctx is not None:
        ck, cv, li, ch0 = ctx
        assert ch0 % hps == 0
        m = ck.shape[3]
        for arr in (ck, cv):
            in_specs.append(pl.BlockSpec((None, None, hps, m, HEAD_DIM),
                                         lambda bb, h, i: (bb, li, ch0 // hps + h, 0, 0)))
            args.append(arr)
    kern = functools.partial(_dense_kernel, hps=hps, groups=groups, tq=tq, nk=n, tk=tk, has_ctx=ctx is not None,
                             has_sink=sink is not None)
    return pl.pallas_call(
        kern,
        out_shape=jax.ShapeDtypeStruct((nb * n, n_kv * groups * HEAD_DIM), BF16),
        grid=(nb, n_kv // hps, qt),
        in_specs=in_specs,
        out_specs=pl.BlockSpec((tq, hps * groups * HEAD_DIM), lambda bb, h, i: (bb * qt + i, h)),
        compiler_params=_params("parallel", "parallel", "parallel"),
        name="dense_attn",
    )(*args)


def _diff_kernel(*refs, hps, nk, tk, has_ctx, lam_init):
    q_ref, k_ref, v_ref = refs[:3]
    pos = 3
    if has_ctx:
        kc_ref, vc_ref = refs[pos:pos + 2]
        pos += 2
    lam_ref, gain_ref, o_ref = refs[pos:pos + 3]

    lb = lam_ref[...]
    lam = (jnp.exp(jnp.sum(lb[0:1] * lb[1:2], axis=-1, keepdims=True))
           - jnp.exp(jnp.sum(lb[2:3] * lb[3:4], axis=-1, keepdims=True)) + lam_init)
    for hh in range(hps):
        q = q_ref[hh]
        lane = lax.broadcasted_iota(jnp.int32, q.shape, 1)
        zero = jnp.zeros_like(q)
        q1 = jnp.where(lane < DIFF_HALF, q, zero)
        q2 = jnp.where(lane >= DIFF_HALF, q, zero)
        st1 = st2 = None
        for c in range(nk // tk):
            kc = k_ref[hh, c * tk:(c + 1) * tk, :]
            vc = _with_ones(v_ref[hh, c * tk:(c + 1) * tk, :])
            st1 = _softmax_step(st1, _dot_nt(q1, kc), vc)
            st2 = _softmax_step(st2, _dot_nt(q2, kc), vc)
        if has_ctx:
            vc = _with_ones(vc_ref[hh])
            st1 = _softmax_step(st1, _dot_nt(q1, kc_ref[hh]), vc)
            st2 = _softmax_step(st2, _dot_nt(q2, kc_ref[hh]), vc)
        o = _normalize(st1[1]) - lam * _normalize(st2[1])
        o = _rmsnorm(o, gain_ref[...]) * (1.0 - lam_init)
        o_ref[:, hh * HEAD_DIM:(hh + 1) * HEAD_DIM] = o.astype(o_ref.dtype)


def _diff_call(qkv, lam_b, subln, li, nb, n, tq, tk, hps=1, ctx=None):
    qt = n // tq
    e = li // 2
    lam_init = 0.8 - 0.6 * math.exp(-0.3 * li)
    assert H_B % hps == 0 and QB % hps == 0 and (ctx is None or HKV_A % hps == 0)
    in_specs = [
        pl.BlockSpec((hps, tq, HEAD_DIM), lambda bb, h, i: (QB // hps + h, bb * qt + i, 0)),
        pl.BlockSpec((hps, n, HEAD_DIM), lambda bb, h, i: (KB // hps + h, bb, 0)),
        pl.BlockSpec((hps, n, HEAD_DIM), lambda bb, h, i: (VB // hps + h, bb, 0)),
    ]
    args = [qkv, qkv, qkv]
    if ctx is not None:
        ck, cv = ctx
        m = ck.shape[3]
        for arr in (ck, cv):
            in_specs.append(pl.BlockSpec((None, None, hps, m, HEAD_DIM),
                                         lambda bb, h, i: (bb, li, HKV_A // hps + h, 0, 0)))
            args.append(arr)
    in_specs += [
        pl.BlockSpec((None, 4, DIFF_HALF), lambda bb, h, i: (e, 0, 0)),
        pl.BlockSpec((None, 1, HEAD_DIM), lambda bb, h, i: (e, 0, 0)),
    ]
    args += [lam_b, subln.reshape(-1, 1, HEAD_DIM)]
    return pl.pallas_call(
        functools.partial(_diff_kernel, hps=hps, nk=n, tk=tk, has_ctx=ctx is not None, lam_init=lam_init),
        out_shape=jax.ShapeDtypeStruct((nb * n, H_B * HEAD_DIM), BF16),
        grid=(nb, H_B // hps, qt),
        in_specs=in_specs,
        out_specs=pl.BlockSpec((tq, hps * HEAD_DIM), lambda bb, h, i: (bb * qt + i, h)),
        compiler_params=_params("parallel", "parallel", "parallel"),
        name="diff_attn",
    )(*args)


NA_Q_ROWS = 4
NA_K_ROWS = NA_Q_ROWS + NA_ROWS


def _na_window_start(j, rows):
    return jnp.clip(NA_Q_ROWS * j - NA_ROWS // 2, 0, rows - NA_K_ROWS)


def _na_bias(rpb, rows):
    nj = rows // NA_Q_ROWS
    heads = rpb.shape[0]
    n_ro, n_co = 2 * NA_ROWS - 1, 2 * NA_COLS - 1
    c = np.arange(GRID_W)[:, None]
    kc = np.arange(GRID_W)[None, :]
    cs = np.clip(c - NA_COLS // 2, 0, GRID_W - NA_COLS)
    col_valid = (kc >= cs) & (kc < cs + NA_COLS)
    onehot = (kc - c + NA_COLS - 1)[None] == np.arange(n_co)[:, None, None]
    rpb2 = rpb.astype(F32) * LOG2E
    band = jnp.sum(jnp.where(onehot[None, None], rpb2[:, :, :, None, None], 0.0), axis=2)
    band = jnp.where(col_valid[None, None], band, MASK_VALUE)
    blocks = jnp.concatenate([band, jnp.full((heads, 1, GRID_W, GRID_W), MASK_VALUE, F32)], axis=1)
    pick = np.full((3, NA_Q_ROWS, NA_K_ROWS), n_ro, np.int32)
    for v, j in enumerate((0, 1, nj - 1)):
        ws = int(np.clip(NA_Q_ROWS * j - NA_ROWS // 2, 0, rows - NA_K_ROWS))
        for rq in range(NA_Q_ROWS):
            r = NA_Q_ROWS * j + rq
            rs = int(np.clip(r - NA_ROWS // 2, 0, rows - NA_ROWS))
            for kr in range(NA_K_ROWS):
                if rs <= ws + kr < rs + NA_ROWS:
                    pick[v, rq, kr] = ws + kr - r + NA_ROWS - 1
    rows_of_blocks = []
    for v in range(3):
        for rq in range(NA_Q_ROWS):
            rows_of_blocks.append(jnp.concatenate([blocks[:, int(pick[v, rq, kr])] for kr in range(NA_K_ROWS)], axis=-1))
    table = jnp.stack(rows_of_blocks, axis=1)
    return table.reshape(heads, 3, NA_Q_ROWS * GRID_W, NA_K_ROWS * GRID_W)


def _windowed_softmax(q, k_loc, v_loc, k_ctx, v_ctx, bias=None, mask=None, sink=None):
    s_loc = _dot_nt(q, k_loc)
    if bias is not None:
        s_loc = s_loc + bias
    if mask is not None:
        s_loc = jnp.where(mask, s_loc, MASK_VALUE)
    s_ctx = _dot_nt(q, k_ctx)
    m = jnp.maximum(jnp.max(s_loc, axis=-1, keepdims=True), jnp.max(s_ctx, axis=-1, keepdims=True))
    if sink is not None:
        m = jnp.maximum(m, sink)
    p_loc = jnp.exp2(s_loc - m)
    p_ctx = jnp.exp2(s_ctx - m)
    l = jnp.sum(p_loc, axis=-1, keepdims=True) + jnp.sum(p_ctx, axis=-1, keepdims=True)
    if sink is not None:
        l = l + jnp.exp2(sink - m)
    return (_dot(p_loc.astype(BF16), v_loc) + _dot(p_ctx.astype(BF16), v_ctx)) / l


def _na_kernel(q_ref, k_ref, v_ref, kc_ref, vc_ref, bias_ref, o_ref):
    for h in range(H_C):
        o = _windowed_softmax(q_ref[h], k_ref[h], v_ref[h], kc_ref[h], vc_ref[h], bias=bias_ref[h])
        o_ref[:, h * HEAD_DIM:(h + 1) * HEAD_DIM] = o.astype(o_ref.dtype)


def _na_call(qkv, bias, ck, cv, li, nb, n):
    rows = n // GRID_W
    nj = rows // NA_Q_ROWS
    tq = NA_Q_ROWS * GRID_W
    span = NA_K_ROWS * GRID_W
    m = ck.shape[3]

    window = (pl.Element(H_C), pl.Element(span), pl.Element(HEAD_DIM))

    def window_map(first_head):
        return lambda bb, j: (first_head, pl.multiple_of(bb * n + _na_window_start(j, rows) * GRID_W, GRID_W), 0)

    def bias_map(bb, j):
        return (0, jnp.where(j == 0, 0, jnp.where(j == nj - 1, 2, 1)), 0, 0)

    return pl.pallas_call(
        _na_kernel,
        out_shape=jax.ShapeDtypeStruct((nb * n, H_C * HEAD_DIM), BF16),
        grid=(nb, nj),
        in_specs=[
            pl.BlockSpec((H_C, tq, HEAD_DIM), lambda bb, j: (QC // H_C, bb * nj + j, 0)),
            pl.BlockSpec(window, window_map(KC)),
            pl.BlockSpec(window, window_map(VC)),
            pl.BlockSpec((None, None, H_C, m, HEAD_DIM), lambda bb, j: (bb, li, 0, 0, 0)),
            pl.BlockSpec((None, None, H_C, m, HEAD_DIM), lambda bb, j: (bb, li, 0, 0, 0)),
            pl.BlockSpec((H_C, None, tq, span), bias_map),
        ],
        out_specs=pl.BlockSpec((tq, H_C * HEAD_DIM), lambda bb, j: (bb * nj + j, 0)),
        compiler_params=_params("parallel", "parallel"),
        name="na2d_attn",
    )(qkv, qkv, qkv, ck, cv, bias)


BAND_SPAN = 3 * WINDOW
BAND_GROUPS = H_D // HKV_D


BAND_BLOCKS_PER_STEP = 4


def _band_kernel(sink_ref, q_ref, k_ref, v_ref, kc_ref, vc_ref, o_ref, *, n):
    rows = BAND_GROUPS * WINDOW
    for s in range(BAND_BLOCKS_PER_STEP):
        blk = pl.program_id(1) * BAND_BLOCKS_PER_STEP + s
        start = pl.multiple_of(jnp.clip((blk - 1) * WINDOW, 0, n - BAND_SPAN), WINDOW)
        qpos = blk * WINDOW + lax.rem(lax.broadcasted_iota(jnp.int32, (rows, BAND_SPAN), 0), WINDOW)
        kpos = start + lax.broadcasted_iota(jnp.int32, (rows, BAND_SPAN), 1)
        in_band = jnp.abs(qpos - kpos) <= WINDOW
        tok = slice(s * WINDOW, (s + 1) * WINDOW)
        for h in range(HKV_D):
            q = q_ref[h * BAND_GROUPS:(h + 1) * BAND_GROUPS, tok, :].reshape(rows, HEAD_DIM)
            sink = _sink_column(sink_ref, h * BAND_GROUPS, BAND_GROUPS, WINDOW)
            o = _windowed_softmax(q, k_ref[h, pl.ds(start, BAND_SPAN), :], v_ref[h, pl.ds(start, BAND_SPAN), :],
                                  kc_ref[h], vc_ref[h], mask=in_band, sink=sink)
            for g in range(BAND_GROUPS):
                col = (h * BAND_GROUPS + g) * HEAD_DIM
                o_ref[tok, col:col + HEAD_DIM] = o[g * WINDOW:(g + 1) * WINDOW, :].astype(o_ref.dtype)


def _band_call(qkv, sink, ck, cv, li, nb, n):
    tq = BAND_BLOCKS_PER_STEP * WINDOW
    nblk = n // tq
    m = ck.shape[3]
    return pl.pallas_call(
        functools.partial(_band_kernel, n=n),
        out_shape=jax.ShapeDtypeStruct((nb * n, H_D * HEAD_DIM), BF16),
        grid=(nb, nblk),
        in_specs=[
            pl.BlockSpec(memory_space=pltpu.SMEM),
            pl.BlockSpec((H_D, tq, HEAD_DIM), lambda bb, i: (QD // H_D, bb * nblk + i, 0)),
            pl.BlockSpec((HKV_D, n, HEAD_DIM), lambda bb, i: (KD // HKV_D, bb, 0)),
            pl.BlockSpec((HKV_D, n, HEAD_DIM), lambda bb, i: (VD // HKV_D, bb, 0)),
            pl.BlockSpec((None, None, HKV_D, m, HEAD_DIM), lambda bb, i: (bb, li, H_C // HKV_D, 0, 0)),
            pl.BlockSpec((None, None, HKV_D, m, HEAD_DIM), lambda bb, i: (bb, li, H_C // HKV_D, 0, 0)),
        ],
        out_specs=pl.BlockSpec((tq, H_D * HEAD_DIM), lambda bb, i: (bb * nblk + i, 0)),
        compiler_params=_params("parallel", "parallel"),
        name="band_attn",
    )(sink, qkv, qkv, qkv, ck, cv)


def _outproj_kernel(x_ref, mod_ref, ma_ref, mb_ref, w_ref, g_ref, b_ref, o_ref):
    gate = mod_ref[0, 5:6, :]
    for r0 in range(0, o_ref.shape[1], OUTPROJ_ROW_SLAB):
        rows = slice(r0, r0 + OUTPROJ_ROW_SLAB)
        f = _dot(ma_ref[rows, :], w_ref[:MIX_HALF, :]) + _dot(mb_ref[rows, :], w_ref[MIX_HALF:, :])
        o_ref[0, rows, :] = _layernorm(DN_ALPHA * x_ref[0, rows, :] + gate * f, g_ref[...], b_ref[...])


def _outproj_call(x, mods, mix_a, mix_b, w_out, ln_g, ln_b, li, tm=1024):
    nb, n, d = x.shape
    tm = min(tm, n)
    assert n % tm == 0 and tm % OUTPROJ_ROW_SLAB == 0
    tiles = n // tm
    return pl.pallas_call(
        _outproj_kernel,
        out_shape=jax.ShapeDtypeStruct(x.shape, F32),
        grid=(nb, tiles),
        in_specs=[
            pl.BlockSpec((1, tm, d), lambda bb, i: (bb, i, 0)),
            pl.BlockSpec((1, N_MOD, d), lambda bb, i: (bb, 0, 0)),
            pl.BlockSpec((tm, MIX_HALF), lambda bb, i: (bb * tiles + i, 0)),
            pl.BlockSpec((tm, MIX_HALF), lambda bb, i: (bb * tiles + i, 0)),
            pl.BlockSpec((None, 2 * MIX_HALF, d), lambda bb, i: (li, 0, 0), pipeline_mode=pl.Buffered(1)),
            pl.BlockSpec((1, d), lambda bb, i: (0, 0)),
            pl.BlockSpec((1, d), lambda bb, i: (0, 0)),
        ],
        out_specs=pl.BlockSpec((1, tm, d), lambda bb, i: (bb, i, 0)),
        compiler_params=_params("parallel", "parallel"),
        name="out_proj",
    )(x, mods, mix_a, mix_b, w_out, ln_g[li, 1][None, :], ln_b[li, 1][None, :])


def _rope_tables(n):
    t = jnp.arange(n, dtype=jnp.int32)
    rowcol = jnp.stack([t // GRID_W, t % GRID_W], axis=-1).astype(F32)
    lane = np.arange(HEAD_DIM)
    out = []
    for half in (32, 16):
        which = (lane // (2 * half)) % 2
        idx = lane % half
        low = (lane % (2 * half)) < half
        freqs = ROPE_THETA ** (-jnp.arange(half, dtype=F32) / half)
        ang = rowcol[:, which] * freqs[idx][None, :]
        cos, sin = jnp.cos(ang), jnp.sin(ang)
        out += [cos, jnp.where(low[None, :], -sin, 0.0), jnp.where(low[None, :], 0.0, sin)]
    return jnp.stack(out, axis=0)


def kernel(x_prompt, x_sample, cache_k, cache_v, c, c_ctx, w_ada, b_ada, ln_g, ln_b, ffn_w1, ffn_w3, ffn_w2,
           w_in, w_out, qk_gain_a, lam_b, subln_b, rpb_c, sink_d):
    batch, seq, d = x_prompt.shape
    dec_batch, dec_seq, _ = x_sample.shape

    w1, w3, w2 = ffn_w1.astype(BF16), ffn_w3.astype(BF16), ffn_w2.astype(BF16)
    w_in_b, w_out_b = w_in.astype(BF16), w_out.astype(BF16)
    ck_b, cv_b = cache_k.astype(BF16), cache_v.astype(BF16)

    assert dec_batch + 1 <= COND_ROWS
    cond = jnp.zeros((COND_ROWS, d), F32).at[:dec_batch].set(c).at[dec_batch].set(c_ctx)
    mods = _mods_call(cond, w_ada, b_ada).reshape(DEPTH, COND_ROWS, N_MOD, d)

    rope = _rope_tables(dec_seq)
    rows = dec_seq // GRID_W
    no_gain = jnp.ones((2, HEAD_DIM), F32)
    ctx_diff_hps = math.gcd(H_B, QB)

    xp = x_prompt.reshape(1, batch * seq, d)
    xs = x_sample
    new_k, new_v = [], []
    for li in range(DEPTH):
        even = li % 2 == 0
        e = li // 2
        mods_s = mods[li, :dec_batch]
        mods_p = mods[li, dec_batch:dec_batch + 1]
        gains = qk_gain_a[e] if even else no_gain

        xp = _ffn_call(xp, mods_p, w1, w3, w2, ln_g, ln_b, li, 0)
        qkv, ck_new, cv_new = _proj_call(xp, mods_p, w_in_b, gains, None, li, seq=seq)
        new_k.append(ck_new)
        new_v.append(cv_new)
        if even:
            mix_a = _dense_call(qkv, QA, KA, VA, HKV_A, H_A // HKV_A, batch, seq, seq, seq, hps=HKV_A)
            mix_b = _diff_call(qkv, lam_b, subln_b, li, batch, seq, seq, seq, hps=ctx_diff_hps)
        else:
            mix_a = _dense_call(qkv, QC, KC, VC, H_C, 1, batch, seq, seq, seq, hps=H_C)
            mix_b = _dense_call(qkv, QD, KD, VD, HKV_D, H_D // HKV_D, batch, seq, seq, seq, hps=HKV_D,
                                sink=sink_d[e])
        xp = _outproj_call(xp, mods_p, mix_a, mix_b, w_out_b, ln_g, ln_b, li)
        xp = _ffn_call(xp, mods_p, w1, w3, w2, ln_g, ln_b, li, 1)

        xs = _ffn_call(xs, mods_s, w1, w3, w2, ln_g, ln_b, li, 0)
        (qkv,) = _proj_call(xs, mods_s, w_in_b, gains, rope, li)
        if even:
            mix_a = _dense_call(qkv, QA, KA, VA, HKV_A, H_A // HKV_A, dec_batch, dec_seq, *LATENT_DENSE_TILES,
                                ctx=(ck_b, cv_b, li, 0))
            mix_b = _diff_call(qkv, lam_b, subln_b, li, dec_batch, dec_seq, *LATENT_DIFF_TILES, ctx=(ck_b, cv_b))
        else:
            mix_a = _na_call(qkv, _na_bias(rpb_c[e], rows), ck_b, cv_b, li, dec_batch, dec_seq)
            mix_b = _band_call(qkv, sink_d[e], ck_b, cv_b, li, dec_batch, dec_seq)
        xs = _outproj_call(xs, mods_s, mix_a, mix_b, w_out_b, ln_g, ln_b, li)
        xs = _ffn_call(xs, mods_s, w1, w3, w2, ln_g, ln_b, li, 1)

    y_prompt = xp.reshape(batch, seq, d)
    new_cache_k = jnp.stack(new_k, axis=1)
    new_cache_v = jnp.stack(new_v, axis=1)
    return (y_prompt, xs, new_cache_k, new_cache_v)
```
